```python
import jax, jax.numpy as jnp
from jax import lax
import numpy as np

D_MODEL = 1024
BATCH = 8
SEQ = 2048
DEPTH = 4

MIX_WIDTH = D_MODEL
A_HEADS = 4
A_DV = MIX_WIDTH // 2 // A_HEADS
A_DK = A_DV // 2
B_HEADS = 4
B_DV = MIX_WIDTH // 2 // B_HEADS
B_DK = B_DV // 2
A_QK = A_HEADS * A_DK
A_V = A_HEADS * A_DV
B_QK = B_HEADS * B_DK
B_V = B_HEADS * B_DV
GLA_RANK = 16
GLA_GATE_TEMP = 16.0
CONV_W = 3
D_FF = 2816
CHUNK = 64
EPS = 1e-6
IN_WIDTHS = (2 * A_QK, A_V, A_V, 4 * A_HEADS, B_QK, B_QK, B_V, B_V, 2 * GLA_RANK)
IN_COLS = 2 * A_QK + A_V + A_V + 4 * A_HEADS + B_QK + B_QK + B_V + B_V + 2 * GLA_RANK

kernel_name = "hybrid_mlstm_gla_convffn_encoder"


def rms_norm(x, g):
    xf = x.astype(jnp.float32)
    y = xf * lax.rsqrt(jnp.mean(xf * xf, axis=-1, keepdims=True) + EPS)
    return (y * g.astype(jnp.float32)).astype(x.dtype)


def head_rms_norm(y, g):
    y = y * lax.rsqrt(jnp.mean(y * y, axis=-1, keepdims=True) + EPS)
    return y * g.astype(jnp.float32)


def dwconv3(x, w, b):
    xp = jnp.pad(x, ((0, 0), (1, 1), (0, 0)))
    return xp[:, :-2] * w[0] + xp[:, 1:-1] * w[1] + xp[:, 2:] * w[2] + b


def flip(t):
    return t[:, ::-1]


def to_chunks(t):
    b, s, h = t.shape[:3]
    t = t.reshape((b, s // CHUNK, CHUNK, h) + t.shape[3:])
    return jnp.moveaxis(t, (1, 3), (0, 2))


def from_chunks(y):
    y = jnp.moveaxis(y, (0, 2), (1, 3))
    b, nc, l, h, d = y.shape
    return y.reshape(b, nc * l, h, d)


def mlstm_chunkwise(q, k, v, i_pre, log_f):
    bsz, _, h, dk = q.shape
    dv = v.shape[-1]
    causal = jnp.tril(jnp.ones((CHUNK, CHUNK), dtype=bool))

    def step(carry, inp):
        c_st, n_st, m_st = carry
        q_, k_, v_, i_, f_ = inp
        b = jnp.cumsum(f_, axis=-1)
        d_intra = jnp.where(causal, b[..., :, None] - b[..., None, :] + i_[..., None, :], -jnp.inf)
        m_inter = b + m_st[..., None]
        m_t = jnp.maximum(m_inter, jnp.max(d_intra, axis=-1))
        s = jnp.einsum('bhtd,bhsd->bhts', q_, k_) * jnp.exp(d_intra - m_t[..., None])
        inter = jnp.exp(m_inter - m_t)
        num = jnp.einsum('bhts,bhse->bhte', s, v_) + inter[..., None] * jnp.einsum('bhtd,bhde->bhte', q_, c_st)
        den = jnp.sum(s, axis=-1) + inter * jnp.einsum('bhtd,bhd->bht', q_, n_st)
        h_out = num / jnp.maximum(jnp.abs(den), jnp.exp(-m_t))[..., None]
        b_last = b[..., -1]
        g = b_last[..., None] - b + i_
        m_new = jnp.maximum(b_last + m_st, jnp.max(g, axis=-1))
        decay = jnp.exp(b_last + m_st - m_new)
        wk = jnp.exp(g - m_new[..., None])[..., None] * k_
        c_new = decay[..., None, None] * c_st + jnp.einsum('bhsd,bhse->bhde', wk, v_)
        n_new = decay[..., None] * n_st + jnp.sum(wk, axis=-2)
        return (c_new, n_new, m_new), h_out

    init = (jnp.zeros((bsz, h, dk, dv), jnp.float32),
            jnp.zeros((bsz, h, dk), jnp.float32),
            jnp.zeros((bsz, h), jnp.float32))
    _, hs = lax.scan(step, init, (to_chunks(q), to_chunks(k), to_chunks(v),
                                  to_chunks(i_pre), to_chunks(log_f)))
    return from_chunks(hs)


def gla_chunkwise(q, k, v, log_a):
    bsz, _, h, dk = q.shape
    dv = v.shape[-1]
    causal = jnp.tril(jnp.ones((CHUNK, CHUNK), dtype=bool))[..., None]

    def step(s_st, inp):
        q_, k_, v_, a_ = inp
        b = jnp.cumsum(a_, axis=-2)
        diff = b[..., :, None, :] - b[..., None, :, :]
        decay = jnp.exp(jnp.where(causal, diff, -jnp.inf))
        s = jnp.einsum('bhtd,bhsd,bhtsd->bhts', q_, k_, decay)
        o = jnp.einsum('bhts,bhse->bhte', s, v_) + jnp.einsum('bhtd,bhde->bhte', q_ * jnp.exp(b), s_st)
        b_last = b[..., -1:, :]
        k_dec = k_ * jnp.exp(b_last - b)
        s_new = jnp.exp(b_last[..., 0, :])[..., None] * s_st + jnp.einsum('bhsd,bhse->bhde', k_dec, v_)
        return s_new, o

    init = jnp.zeros((bsz, h, dk, dv), jnp.float32)
    _, os_ = lax.scan(step, init, (to_chunks(q), to_chunks(k), to_chunks(v), to_chunks(log_a)))
    return from_chunks(os_)


def hybrid_mixer(hn, w_in, mlstm_gate_b, mlstm_conv_w, mlstm_conv_b, mlstm_norm,
                 gla_w2, gla_b, gla_norm, w_out):
    bsz, s, _ = hn.shape
    proj = jnp.matmul(hn, w_in).astype(jnp.float32)
    idx = np.cumsum(np.array(IN_WIDTHS))[:-1].tolist()
    a_qk, a_v, a_o, a_gates, b_q, b_k, b_v, b_r, b_lr = jnp.split(proj, idx, axis=-1)

    qk = jax.nn.silu(dwconv3(a_qk, mlstm_conv_w.astype(jnp.float32), mlstm_conv_b.astype(jnp.float32)))
    q_a = qk[..., :A_QK].reshape(bsz, s, A_HEADS, A_DK)
    k_a = qk[..., A_QK:].reshape(bsz, s, A_HEADS, A_DK) * (A_DK ** -0.5)
    v_a = a_v.reshape(bsz, s, A_HEADS, A_DV)
    gates = a_gates + mlstm_gate_b.astype(jnp.float32)
    i_fw, f_fw, i_bw, f_bw = jnp.split(gates, 4, axis=-1)
    h_fw = mlstm_chunkwise(q_a, k_a, v_a, i_fw, jax.nn.log_sigmoid(f_fw))
    h_bw = flip(mlstm_chunkwise(flip(q_a), flip(k_a), flip(v_a), flip(i_bw), flip(jax.nn.log_sigmoid(f_bw))))
    y_a = jax.nn.sigmoid(a_o) * head_rms_norm(h_fw + h_bw, mlstm_norm).reshape(bsz, s, A_V)

    q_b = b_q.reshape(bsz, s, B_HEADS, B_DK) * (B_DK ** -0.5)
    k_b = b_k.reshape(bsz, s, B_HEADS, B_DK)
    v_b = b_v.reshape(bsz, s, B_HEADS, B_DV)
    lr_fw, lr_bw = jnp.split(b_lr, 2, axis=-1)
    w2 = gla_w2.astype(jnp.float32)
    gb = gla_b.astype(jnp.float32)
    la_fw = (jax.nn.log_sigmoid(lr_fw @ w2[0] + gb[0]) / GLA_GATE_TEMP).reshape(bsz, s, B_HEADS, B_DK)
    la_bw = (jax.nn.log_sigmoid(lr_bw @ w2[1] + gb[1]) / GLA_GATE_TEMP).reshape(bsz, s, B_HEADS, B_DK)
    o_fw = gla_chunkwise(q_b, k_b, v_b, la_fw)
    o_bw = flip(gla_chunkwise(flip(q_b), flip(k_b), flip(v_b), flip(la_bw)))
    y_b = jax.nn.silu(b_r) * head_rms_norm(o_fw + o_bw, gla_norm).reshape(bsz, s, B_V)

    y = jnp.concatenate([y_a, y_b], axis=-1).astype(hn.dtype)
    return jnp.matmul(y, w_out)


def conv_ffn(hn, w_gate, w_up, conv_w, conv_b, w_down):
    g = dwconv3(jnp.matmul(hn, w_gate), conv_w, conv_b)
    u = jnp.matmul(hn, w_up)
    return jnp.matmul(jax.nn.gelu(g, approximate=True) * u, w_down)


def setup_inputs(seed: int = 0) -> dict:
    key = jax.random.key(seed)
    ks = jax.random.split(key, 20)
    f32 = jnp.float32

    def nrm(k, shape, scale):
        return jax.random.normal(k, shape, f32) * scale

    def gain(k, shape):
        return 1.0 + 0.05 * jax.random.normal(k, shape, f32)

    f_base = jnp.linspace(3.0, 6.0, A_HEADS, dtype=f32)
    i_base = jnp.zeros((A_HEADS,), f32)
    gate_base = jnp.concatenate([i_base, f_base, i_base, f_base])
    return {
        "x": nrm(ks[0], (BATCH, SEQ, D_MODEL), 1.0),
        "norm_mix_pre": gain(ks[1], (DEPTH, D_MODEL)),
        "norm_mix_post": gain(ks[2], (DEPTH, D_MODEL)),
        "norm_ffn_pre": gain(ks[3], (DEPTH, D_MODEL)),
        "norm_ffn_post": gain(ks[4], (DEPTH, D_MODEL)),
        "w_in": nrm(ks[5], (DEPTH, D_MODEL, IN_COLS), D_MODEL ** -0.5),
        "mlstm_gate_b": gate_base + nrm(ks[6], (DEPTH, 4 * A_HEADS), 0.1),
        "mlstm_conv_w": nrm(ks[7], (DEPTH, CONV_W, 2 * A_QK), CONV_W ** -0.5),
        "mlstm_conv_b": nrm(ks[8], (DEPTH, 2 * A_QK), 0.02),
        "mlstm_norm": gain(ks[9], (DEPTH, A_HEADS, A_DV)),
        "gla_w2": nrm(ks[10], (DEPTH, 2, GLA_RANK, B_QK), GLA_RANK ** -0.5),
        "gla_b": nrm(ks[11], (DEPTH, 2, B_QK), 0.02),
        "gla_norm": gain(ks[12], (DEPTH, B_HEADS, B_DV)),
        "w_out": nrm(ks[13], (DEPTH, MIX_WIDTH, D_MODEL), MIX_WIDTH ** -0.5),
        "ffn_w_gate": nrm(ks[14], (DEPTH, D_MODEL, D_FF), D_MODEL ** -0.5),
        "ffn_w_up": nrm(ks[15], (DEPTH, D_MODEL, D_FF), D_MODEL ** -0.5),
        "ffn_conv_w": nrm(ks[16], (DEPTH, CONV_W, D_FF), CONV_W ** -0.5),
        "ffn_conv_b": nrm(ks[17], (DEPTH, D_FF), 0.02),
        "ffn_w_down": nrm(ks[18], (DEPTH, D_FF, D_MODEL), D_FF ** -0.5),
    }


def reference(x, norm_mix_pre, norm_mix_post, norm_ffn_pre, norm_ffn_post, w_in,
              mlstm_gate_b, mlstm_conv_w, mlstm_conv_b, mlstm_norm, gla_w2, gla_b, gla_norm,
              w_out, ffn_w_gate, ffn_w_up, ffn_conv_w, ffn_conv_b, ffn_w_down):
    for l in range(DEPTH):
        hn = rms_norm(x, norm_mix_pre[l])
        mix = hybrid_mixer(hn, w_in[l], mlstm_gate_b[l], mlstm_conv_w[l], mlstm_conv_b[l], mlstm_norm[l],
                           gla_w2[l], gla_b[l], gla_norm[l], w_out[l])
        x = x + rms_norm(mix, norm_mix_post[l])
        hn = rms_norm(x, norm_ffn_pre[l])
        ff = conv_ffn(hn, ffn_w_gate[l], ffn_w_up[l], ffn_conv_w[l], ffn_conv_b[l], ffn_w_down[l])
        x = x + rms_norm(ff, norm_ffn_post[l])
    return x
```

```python
import functools

import jax
import jax.numpy as jnp
from jax import lax
from jax.experimental import pallas as pl
from jax.experimental.pallas import tpu as pltpu

F32 = jnp.float32
BF16 = jnp.bfloat16

D_MODEL = 1024
BATCH = 8
SEQ = 2048
DEPTH = 4
TOKENS = BATCH * SEQ
HEADS = 4
DK = 64
DV = 128
QK_W = HEADS * DK
V_W = HEADS * DV
GLA_RANK = 16
GLA_GATE_TEMP = 16.0
D_FF = 2816
EPS = 1e-6
IN_WIDTHS = (2 * QK_W, V_W, V_W, 4 * HEADS, QK_W, QK_W, V_W, V_W, 2 * GLA_RANK)
N_MAIN = 2 * QK_W + V_W + V_W + QK_W + QK_W + V_W + V_W
N_SMALL = 4 * HEADS + 2 * GLA_RANK
SMALL_PAD = 128

LANES = 128
CHUNK = 128
NCHUNK = SEQ // CHUNK
ROW_BLK = 256
TM = 512
FF_TN = 256
GLA_LEVELS = (64, 32, 16, 8, 4, 2)
VMEM_LIMIT = 56 * 1024 * 1024


def _dot(a, b):
    return jnp.dot(a, b, preferred_element_type=F32)


def _dot_nt(a, b):
    return lax.dot_general(a, b, (((1,), (1,)), ((), ())), preferred_element_type=F32)


def _dot_tn(a, b):
    return lax.dot_general(a, b, (((0,), (0,)), ((), ())), preferred_element_type=F32)


def _split_bf16(x):
    hi = x.astype(BF16)
    r1 = x - hi.astype(F32)
    mid = r1.astype(BF16)
    lo = (r1 - mid.astype(F32)).astype(BF16)
    return hi, mid, lo


def _sel_dot(sel, x):
    hi, mid, lo = _split_bf16(x)
    return _dot(sel, hi) + _dot(sel, mid) + _dot(sel, lo)


def _dot_sel(x, sel):
    hi, mid, lo = _split_bf16(x)
    return _dot(hi, sel) + _dot(mid, sel) + _dot(lo, sel)


def _log_sigmoid(x):
    return jnp.minimum(x, 0.0) - jnp.log1p(jnp.exp(-jnp.abs(x)))


def _sigmoid(x):
    return 1.0 / (1.0 + jnp.exp(-x))


def _rms_scale(x):
    return lax.rsqrt(jnp.mean(x * x, axis=-1, keepdims=True) + EPS)


def _norm_inproj_kernel(x_ref, g_ref, wm_ref, ws_ref, wst_ref, main_ref, small_ref, smallt_ref):
    x = x_ref[...]
    hn = (x * _rms_scale(x) * g_ref[...]).astype(BF16)
    for j in range(N_MAIN // 512):
        cols = slice(j * 512, (j + 1) * 512)
        main_ref[:, cols] = _dot(hn, wm_ref[:, cols]).astype(BF16)
    small_ref[...] = _dot(hn, ws_ref[...])
    st = _dot_nt(wst_ref[...], hn)
    for j in range(TM // CHUNK):
        smallt_ref[j] = st[:, j * CHUNK:(j + 1) * CHUNK]


def _norm_inproj(x2d, g, w_main, w_small, w_small_t):
    return pl.pallas_call(
        _norm_inproj_kernel,
        grid=(TOKENS // TM,),
        in_specs=[
            pl.BlockSpec((TM, D_MODEL), lambda i: (i, 0)),
            pl.BlockSpec((1, D_MODEL), lambda i: (0, 0)),
            pl.BlockSpec((D_MODEL, N_MAIN), lambda i: (0, 0)),
            pl.BlockSpec((D_MODEL, SMALL_PAD), lambda i: (0, 0)),
            pl.BlockSpec((N_SMALL, D_MODEL), lambda i: (0, 0)),
        ],
        out_specs=[
            pl.BlockSpec((TM, N_MAIN), lambda i: (i, 0)),
            pl.BlockSpec((TM, SMALL_PAD), lambda i: (i, 0)),
            pl.BlockSpec((TM // CHUNK, N_SMALL, CHUNK), lambda i: (i, 0, 0)),
        ],
        out_shape=[
            jax.ShapeDtypeStruct((TOKENS, N_MAIN), BF16),
            jax.ShapeDtypeStruct((TOKENS, SMALL_PAD), F32),
            jax.ShapeDtypeStruct((TOKENS // CHUNK, N_SMALL, CHUNK), F32),
        ],
        compiler_params=pltpu.CompilerParams(
            dimension_semantics=("arbitrary",), vmem_limit_bytes=VMEM_LIMIT),
        name="norm_inproj",
    )(x2d, g, w_main, w_small, w_small_t)


def _tri_masks():
    t = lax.broadcasted_iota(jnp.int32, (CHUNK, CHUNK), 0)
    s = lax.broadcasted_iota(jnp.int32, (CHUNK, CHUNK), 1)
    return t, s


def _blk(idx, m):
    return jnp.right_shift(idx, m.bit_length() - 1)


def _half_mask(hh):
    lane = lax.broadcasted_iota(jnp.int32, (CHUNK, LANES), 1)
    return (lane < DK) if hh == 0 else (lane >= DK)


def _head_norm_gate_pass(sum_sc, gate_ref, nw_ref, y_ref, gate_fn):
    def body(r, carry):
        rows = pl.ds(pl.multiple_of(r * ROW_BLK, ROW_BLK), ROW_BLK)
        for h in range(HEADS):
            cols = slice(h * DV, (h + 1) * DV)
            hs = sum_sc[rows, cols]
            yn = hs * _rms_scale(hs) * nw_ref[:, cols]
            gate = gate_ref[rows, cols].astype(F32)
            y_ref[rows, cols] = (gate_fn(gate) * yn).astype(BF16)
        return carry
    lax.fori_loop(0, SEQ // ROW_BLK, body, 0)


def _mlstm_kernel(aqk_ref, av_ref, ao_ref, gcol_ref, grow_ref, gbrow_ref, gbcol_ref, cw_ref, cb_ref, nw_ref,
                  y_ref, q_sc, k_sc, hsum_sc, c_sc):

    def conv_body(r, carry):
        r0 = pl.multiple_of(r * ROW_BLK, ROW_BLK)
        cur = aqk_ref[pl.ds(r0, ROW_BLK), :].astype(F32)
        p0 = pl.multiple_of(jnp.maximum(r0 - 16, 0), 16)
        n0 = pl.multiple_of(jnp.minimum(r0 + ROW_BLK, SEQ - 16), 16)
        prev_row = aqk_ref[pl.ds(p0, 16), :].astype(F32)[15:16]
        next_row = aqk_ref[pl.ds(n0, 16), :].astype(F32)[0:1]
        prev_row = jnp.where(r == 0, 0.0, prev_row)
        next_row = jnp.where(r == SEQ // ROW_BLK - 1, 0.0, next_row)
        row = lax.broadcasted_iota(jnp.int32, (ROW_BLK, 1), 0)
        prev = jnp.where(row == 0, prev_row, pltpu.roll(cur, 1, 0))
        nxt = jnp.where(row == ROW_BLK - 1, next_row, pltpu.roll(cur, ROW_BLK - 1, 0))
        c = prev * cw_ref[0:1, :] + cur * cw_ref[1:2, :] + nxt * cw_ref[2:3, :] + cb_ref[...]
        act = c * _sigmoid(c)
        q_sc[pl.ds(r0, ROW_BLK), :] = act[:, :QK_W].astype(BF16)
        k_sc[pl.ds(r0, ROW_BLK), :] = (act[:, QK_W:] * (DK ** -0.5)).astype(BF16)
        return carry
    lax.fori_loop(0, SEQ // ROW_BLK, conv_body, 0)

    hsum_sc[...] = jnp.zeros_like(hsum_sc)
    c_sc[...] = jnp.zeros_like(c_sc)

    t_idx, s_idx = _tri_masks()
    le = t_idx >= s_idx
    ge = t_idx <= s_idx
    tri_le = le.astype(BF16)
    tri_ge = ge.astype(BF16)
    ones_v = jnp.ones((CHUNK, DV), BF16)

    def one_dir(c, direction, m_st):
        r0 = pl.multiple_of(c * CHUNK, CHUNK)
        rows = pl.ds(r0, CHUNK)
        mask = le if direction == 0 else ge
        t_last = CHUNK - 1 if direction == 0 else 0
        i_off = 0 if direction == 0 else 2 * HEADS
        f_off = i_off + HEADS

        g_row = grow_ref[c][0:4 * HEADS, :] + gbrow_ref[...]
        lf_row = _log_sigmoid(g_row)
        b_row_all = _dot_sel(lf_row, tri_ge if direction == 0 else tri_le)
        g_col = gcol_ref[rows, :] + gbcol_ref[...]
        lf_col = _log_sigmoid(g_col)
        b_col_all = _sel_dot(tri_le if direction == 0 else tri_ge, lf_col)

        new_m = []
        for p in range(HEADS // 2):
            lanes = slice(p * LANES, (p + 1) * LANES)
            q2 = q_sc[rows, lanes]
            k2 = k_sc[rows, lanes]
            k2f = k2.astype(F32)
            q2f = q2.astype(F32)
            c_pair = c_sc[direction, p]
            c_pair_bf = c_pair.astype(BF16)
            upd_pair = []
            decay_rows = []
            for hh in range(2):
                h = 2 * p + hh
                half = _half_mask(hh)
                m_h = m_st[direction * HEADS + h]
                a_row = g_row[i_off + h:i_off + h + 1, :] - b_row_all[f_off + h:f_off + h + 1, :]
                b_col = b_col_all[:, f_off + h:f_off + h + 1]
                a_col = g_col[:, i_off + h:i_off + h + 1] - b_col
                a_mat = jnp.broadcast_to(a_row, (CHUNK, CHUNK))
                pmax = jnp.max(jnp.where(mask, a_mat, -jnp.inf), axis=1, keepdims=True)
                u = jnp.maximum(m_h, pmax)
                e = jnp.where(mask, jnp.exp(a_mat - u), 0.0)
                qm = jnp.where(half, q2, jnp.zeros_like(q2))
                sraw = _dot_nt(qm, k2)
                pm = (sraw * e).astype(BF16)
                inter = jnp.exp(m_h - u)
                qi = jnp.where(half, q2f * inter, 0.0).astype(BF16)
                v_ext = jnp.concatenate([av_ref[rows, h * DV:(h + 1) * DV], ones_v], axis=1)
                nd = _dot(pm, v_ext) + _dot(qi, c_pair_bf)
                num = nd[:, :DV]
                den = nd[:, DV:]
                thr = jnp.exp(-b_col - u)
                hout = num / jnp.maximum(jnp.abs(den), thr)
                hsum_sc[rows, h * DV:(h + 1) * DV] += hout
                u_last = u[t_last:t_last + 1, :]
                b_last = b_col[t_last:t_last + 1, :]
                new_m.append(b_last + u_last)
                decay = jnp.exp(m_h - u_last)
                w_col = jnp.exp(a_col - u_last)
                wk = jnp.where(half, k2f * w_col, 0.0).astype(BF16)
                upd_pair.append(_dot_tn(wk, v_ext))
                decay_rows.append(jnp.broadcast_to(decay, (DK, 2 * DV)))
            decay_mat = jnp.concatenate(decay_rows, axis=0)
            c_sc[direction, p] = decay_mat * c_pair + upd_pair[0] + upd_pair[1]
        return new_m

    def chunk_body(i, m_st):
        m_fw = one_dir(i, 0, m_st)
        m_bw = one_dir(NCHUNK - 1 - i, 1, m_st)
        return tuple(m_fw + m_bw)

    m0 = tuple(jnp.zeros((1, 1), F32) for _ in range(2 * HEADS))
    lax.fori_loop(0, NCHUNK, chunk_body, m0)

    _head_norm_gate_pass(hsum_sc, ao_ref, nw_ref, y_ref, _sigmoid)


def _mlstm(main, small, small_t, gb_row, gb_col, conv_w, conv_b, norm_w):
    def blk(j):
        return pl.BlockSpec((SEQ, V_W), lambda b: (b, j))
    full = lambda shape: pl.BlockSpec(shape, lambda b: tuple(0 for _ in shape))
    return pl.pallas_call(
        _mlstm_kernel,
        grid=(BATCH,),
        in_specs=[
            blk(0), blk(1), blk(2),
            pl.BlockSpec((SEQ, SMALL_PAD), lambda b: (b, 0)),
            pl.BlockSpec((NCHUNK, N_SMALL, CHUNK), lambda b: (b, 0, 0)),
            full((4 * HEADS, CHUNK)), full((1, SMALL_PAD)), full((3, 2 * QK_W)), full((1, 2 * QK_W)), full((1, V_W)),
        ],
        out_specs=pl.BlockSpec((SEQ, V_W), lambda b: (b, 0)),
        out_shape=jax.ShapeDtypeStruct((TOKENS, V_W), BF16),
        scratch_shapes=[
            pltpu.VMEM((SEQ, QK_W), BF16),
            pltpu.VMEM((SEQ, QK_W), BF16),
            pltpu.VMEM((SEQ, V_W), F32),
            pltpu.VMEM((2, HEADS // 2, 2 * DK, 2 * DV), F32),
        ],
        compiler_params=pltpu.CompilerParams(
            dimension_semantics=("arbitrary",), vmem_limit_bytes=VMEM_LIMIT),
        name="mlstm",
    )(main, main, main, small, small_t, gb_row, gb_col, conv_w, conv_b, norm_w)


def _gla_kernel(bqk_ref, bv_ref, br_ref, gcol_ref, w2_ref, gb_ref, nw_ref,
                y_ref, q_sc, k_sc, la_sc, osum_sc, s_sc):

    def pre_body(r, carry):
        rows = pl.ds(pl.multiple_of(r * ROW_BLK, ROW_BLK), ROW_BLK)
        qk = bqk_ref[rows, :].astype(F32)
        q_sc[rows, :] = (qk[:, :QK_W] * (DK ** -0.5)).astype(BF16)
        k_sc[rows, :] = qk[:, QK_W:].astype(BF16)
        lr = gcol_ref[rows, :].astype(BF16)
        for d in range(2):
            z = _dot(lr, w2_ref[d]) + gb_ref[d]
            la_sc[d, rows, :] = _log_sigmoid(z) * (1.0 / GLA_GATE_TEMP)
        return carry
    lax.fori_loop(0, SEQ // ROW_BLK, pre_body, 0)

    osum_sc[...] = jnp.zeros_like(osum_sc)
    s_sc[...] = jnp.zeros_like(s_sc)

    t_idx, s_idx = _tri_masks()
    eye = t_idx == s_idx

    def one_dir(c, direction):
        r0 = pl.multiple_of(c * CHUNK, CHUNK)
        rows = pl.ds(r0, CHUNK)
        causal = (t_idx >= s_idx) if direction == 0 else (t_idx <= s_idx)
        strict_rev = (s_idx > t_idx) if direction == 0 else (s_idx < t_idx)
        t_last = CHUNK - 1 if direction == 0 else 0

        la = la_sc[direction, rows, :]
        la3 = _split_bf16(la)

        def seg_sum(sel):
            selb = sel.astype(BF16)
            return _dot(selb, la3[0]) + _dot(selb, la3[1]) + _dot(selb, la3[2])

        levels = []
        for m in (CHUNK,) + GLA_LEVELS:
            same = _blk(t_idx, m) == _blk(s_idx, m)
            dq = seg_sum(same & causal)
            dk = seg_sum(same & strict_rev)
            levels.append((m, dq, dk))
        levels.append((1, la, None))

        for p in range(HEADS // 2):
            lanes = slice(p * LANES, (p + 1) * LANES)
            q2 = q_sc[rows, lanes].astype(F32)
            k2 = k_sc[rows, lanes].astype(F32)
            k2b = k2.astype(BF16)
            acc = [None, None]
            for hh in range(2):
                qm = jnp.where(_half_mask(hh), q2, 0.0).astype(BF16)
                acc[hh] = jnp.where(eye, _dot_nt(qm, k2b), 0.0)
            q_inter = None
            k_state = None
            for (m, dq, dk) in levels:
                qt = q2 * jnp.exp(dq[:, lanes])
                kt = k2 if dk is None else k2 * jnp.exp(dk[:, lanes])
                if m == CHUNK:
                    q_inter, k_state = qt, kt
                    continue
                ktb = kt.astype(BF16)
                tb = _blk(t_idx, m)
                sb = _blk(s_idx, m)
                if direction == 0:
                    lvl_mask = ((tb & 1) == 1) & (sb == tb - 1)
                else:
                    lvl_mask = ((tb & 1) == 0) & (sb == tb + 1)
                for hh in range(2):
                    qm = jnp.where(_half_mask(hh), qt, 0.0).astype(BF16)
                    acc[hh] = acc[hh] + jnp.where(lvl_mask, _dot_nt(qm, ktb), 0.0)
            s_pair = s_sc[direction, p]
            s_pair_bf = s_pair.astype(BF16)
            for hh in range(2):
                h = 2 * p + hh
                v_h = bv_ref[rows, h * DV:(h + 1) * DV]
                qi = jnp.where(_half_mask(hh), q_inter, 0.0).astype(BF16)
                o = _dot(acc[hh].astype(BF16), v_h) + _dot(qi, s_pair_bf)
                osum_sc[rows, h * DV:(h + 1) * DV] += o
            v_pair = bv_ref[rows, p * 2 * DV:(p + 1) * 2 * DV]
            upd = _dot_tn(k_state.astype(BF16), v_pair)
            upd_pair = jnp.concatenate([upd[:DK, :DV], upd[DK:, DV:]], axis=0)
            btot_row = levels[0][1][t_last:t_last + 1, lanes]
            btot_col = jnp.sum(jnp.where(eye, jnp.broadcast_to(btot_row, (CHUNK, LANES)), 0.0),
                               axis=1, keepdims=True)
            s_sc[direction, p] = jnp.exp(btot_col) * s_pair + upd_pair

    def chunk_body(i, carry):
        one_dir(i, 0)
        one_dir(NCHUNK - 1 - i, 1)
        return carry

    lax.fori_loop(0, NCHUNK, chunk_body, 0)

    _head_norm_gate_pass(osum_sc, br_ref, nw_ref, y_ref, lambda g: g * _sigmoid(g))


def _gla(main, small, w2, gb, norm_w):
    full = lambda shape: pl.BlockSpec(shape, lambda b: tuple(0 for _ in shape))
    return pl.pallas_call(
        _gla_kernel,
        grid=(BATCH,),
        in_specs=[
            pl.BlockSpec((SEQ, V_W), lambda b: (b, 3)),
            pl.BlockSpec((SEQ, V_W), lambda b: (b, 4)),
            pl.BlockSpec((SEQ, V_W), lambda b: (b, 5)),
            pl.BlockSpec((SEQ, SMALL_PAD), lambda b: (b, 0)),
            full((2, SMALL_PAD, QK_W)), full((2, 1, QK_W)), full((1, V_W)),
        ],
        out_specs=pl.BlockSpec((SEQ, V_W), lambda b: (b, 0)),
        out_shape=jax.ShapeDtypeStruct((TOKENS, V_W), BF16),
        scratch_shapes=[
            pltpu.VMEM((SEQ, QK_W), BF16),
            pltpu.VMEM((SEQ, QK_W), BF16),
            pltpu.VMEM((2, SEQ, QK_W), F32),
            pltpu.VMEM((SEQ, V_W), F32),
            pltpu.VMEM((2, HEADS // 2, 2 * DK, DV), F32),
        ],
        compiler_params=pltpu.CompilerParams(
            dimension_semantics=("arbitrary",), vmem_limit_bytes=VMEM_LIMIT),
        name="gla",
    )(main, main, main, small, w2, gb, norm_w)


def _outproj_kernel(ya_ref, yb_ref, x_ref, w_ref, g_ref, o_ref):
    mix = _dot(ya_ref[...], w_ref[:V_W, :]) + _dot(yb_ref[...], w_ref[V_W:, :])
    o_ref[...] = x_ref[...] + mix * _rms_scale(mix) * g_ref[...]


def _outproj(ya, yb, x2d, w, g):
    return pl.pallas_call(
        _outproj_kernel,
        grid=(TOKENS // TM,),
        in_specs=[
            pl.BlockSpec((TM, V_W), lambda i: (i, 0)),
            pl.BlockSpec((TM, V_W), lambda i: (i, 0)),
            pl.BlockSpec((TM, D_MODEL), lambda i: (i, 0)),
            pl.BlockSpec((D_MODEL, D_MODEL), lambda i: (0, 0)),
            pl.BlockSpec((1, D_MODEL), lambda i: (0, 0)),
        ],
        out_specs=pl.BlockSpec((TM, D_MODEL), lambda i: (i, 0)),
        out_shape=jax.ShapeDtypeStruct((TOKENS, D_MODEL), F32),
        compiler_params=pltpu.CompilerParams(
            dimension_semantics=("arbitrary",), vmem_limit_bytes=VMEM_LIMIT),
        name="outproj",
    )(ya, yb, x2d, w, g)


def _gelu_tanh(x):
    return 0.5 * x * (1.0 + jnp.tanh(0.7978845608028654 * (x + 0.044715 * (x * x * x))))


def _ffn_up_kernel(x_ref, g_ref, wg_ref, wu_ref, cw_ref, cb_ref, h_ref, hn_sc, g_sc):
    n = pl.program_id(1)

    @pl.when(n == 0)
    def _():
        def norm_body(r, carry):
            rows = pl.ds(pl.multiple_of(r * ROW_BLK, ROW_BLK), ROW_BLK)
            x = x_ref[rows, :]
            hn_sc[rows, :] = (x * _rms_scale(x) * g_ref[...]).astype(BF16)
            return carry
        lax.fori_loop(0, SEQ // ROW_BLK, norm_body, 0)

    g_sc[0:8, :] = jnp.zeros((8, FF_TN), F32)
    g_sc[SEQ + 8:SEQ + 16, :] = jnp.zeros((8, FF_TN), F32)

    def gate_body(r, carry):
        r0 = pl.multiple_of(r * ROW_BLK, ROW_BLK)
        g_sc[pl.ds(r0 + 8, ROW_BLK), :] = _dot(hn_sc[pl.ds(r0, ROW_BLK), :], wg_ref[...])
        return carry
    lax.fori_loop(0, SEQ // ROW_BLK, gate_body, 0)

    def act_body(r, carry):
        r0 = pl.multiple_of(r * ROW_BLK, ROW_BLK)
        cur = g_sc[pl.ds(r0 + 8, ROW_BLK), :]
        prev_row = g_sc[pl.ds(r0, 8), :][7:8]
        next_row = g_sc[pl.ds(r0 + 8 + ROW_BLK, 8), :][0:1]
        row = lax.broadcasted_iota(jnp.int32, (ROW_BLK, 1), 0)
        prev = jnp.where(row == 0, prev_row, pltpu.roll(cur, 1, 0))
        nxt = jnp.where(row == ROW_BLK - 1, next_row, pltpu.roll(cur, ROW_BLK - 1, 0))
        gc = prev * cw_ref[0:1, :] + cur * cw_ref[1:2, :] + nxt * cw_ref[2:3, :] + cb_ref[...]
        u = _dot(hn_sc[pl.ds(r0, ROW_BLK), :], wu_ref[...])
        h_ref[pl.ds(r0, ROW_BLK), :] = (_gelu_tanh(gc) * u).astype(BF16)
        return carry
    lax.fori_loop(0, SEQ // ROW_BLK, act_body, 0)


def _ffn_up(x2d, g, wg, wu, cw, cb):
    return pl.pallas_call(
        _ffn_up_kernel,
        grid=(BATCH, D_FF // FF_TN),
        in_specs=[
            pl.BlockSpec((SEQ, D_MODEL), lambda b, n: (b, 0)),
            pl.BlockSpec((1, D_MODEL), lambda b, n: (0, 0)),
            pl.BlockSpec((D_MODEL, FF_TN), lambda b, n: (0, n)),
            pl.BlockSpec((D_MODEL, FF_TN), lambda b, n: (0, n)),
            pl.BlockSpec((3, FF_TN), lambda b, n: (0, n)),
            pl.BlockSpec((1, FF_TN), lambda b, n: (0, n)),
        ],
        out_specs=pl.BlockSpec((SEQ, FF_TN), lambda b, n: (b, n)),
        out_shape=jax.ShapeDtypeStruct((TOKENS, D_FF), BF16),
        scratch_shapes=[
            pltpu.VMEM((SEQ, D_MODEL), BF16),
            pltpu.VMEM((SEQ + 16, FF_TN), F32),
        ],
        compiler_params=pltpu.CompilerParams(
            dimension_semantics=("arbitrary", "arbitrary"), vmem_limit_bytes=VMEM_LIMIT),
        name="ffn_up",
    )(x2d, g, wg, wu, cw, cb)


def _ffn_down_kernel(h_ref, x_ref, w_ref, g_ref, o_ref):
    ff = _dot(h_ref[...], w_ref[...])
    o_ref[...] = x_ref[...] + ff * _rms_scale(ff) * g_ref[...]


def _ffn_down(h, x2d, w, g):
    return pl.pallas_call(
        _ffn_down_kernel,
        grid=(TOKENS // TM,),
        in_specs=[
            pl.BlockSpec((TM, D_FF), lambda i: (i, 0)),
            pl.BlockSpec((TM, D_MODEL), lambda i: (i, 0)),
            pl.BlockSpec((D_FF, D_MODEL), lambda i: (0, 0)),
            pl.BlockSpec((1, D_MODEL), lambda i: (0, 0)),
        ],
        out_specs=pl.BlockSpec((TM, D_MODEL), lambda i: (i, 0)),
        out_shape=jax.ShapeDtypeStruct((TOKENS, D_MODEL), F32),
        compiler_params=pltpu.CompilerParams(
            dimension_semantics=("arbitrary",), vmem_limit_bytes=VMEM_LIMIT),
        name="ffn_down",
    )(h, x2d, w, g)


def _split_in_weights(w_in_l):
    offs = [0]
    for w in IN_WIDTHS:
        offs.append(offs[-1] + w)
    a_qk, a_v, a_o, gates, b_q, b_k, b_v, b_r, b_lr = (w_in_l[:, offs[i]:offs[i + 1]] for i in range(9))
    w_main = jnp.concatenate([a_qk, a_v, a_o, b_q, b_k, b_v, b_r], axis=1).astype(BF16)
    small = jnp.concatenate([gates, b_lr], axis=1)
    w_small = jnp.pad(small, ((0, 0), (0, SMALL_PAD - N_SMALL))).astype(BF16)
    w_small_t = small.T.astype(BF16)
    return w_main, w_small, w_small_t


def kernel(x, norm_mix_pre, norm_mix_post, norm_ffn_pre, norm_ffn_post, w_in, mlstm_gate_b, mlstm_conv_w,
           mlstm_conv_b, mlstm_norm, gla_w2, gla_b, gla_norm, w_out, ffn_w_gate, ffn_w_up, ffn_conv_w,
           ffn_conv_b, ffn_w_down):
    x2d = x.reshape(TOKENS, D_MODEL).astype(F32)
    for l in range(DEPTH):
        w_main, w_small, w_small_t = _split_in_weights(w_in[l])
        main, small, small_t = _norm_inproj(x2d, norm_mix_pre[l].reshape(1, D_MODEL), w_main, w_small, w_small_t)

        gate_b = mlstm_gate_b[l].astype(F32)
        gb_row = jnp.broadcast_to(gate_b[:, None], (4 * HEADS, CHUNK))
        gb_col = jnp.pad(gate_b, (0, SMALL_PAD - 4 * HEADS)).reshape(1, SMALL_PAD)
        y_a = _mlstm(main, small, small_t, gb_row, gb_col,
                     mlstm_conv_w[l].astype(F32), mlstm_conv_b[l].reshape(1, 2 * QK_W).astype(F32),
                     mlstm_norm[l].reshape(1, V_W).astype(F32))

        w2 = jnp.zeros((2, SMALL_PAD, QK_W), F32)
        w2 = w2.at[0, 4 * HEADS:4 * HEADS + GLA_RANK].set(gla_w2[l, 0])
        w2 = w2.at[1, 4 * HEADS + GLA_RANK:4 * HEADS + 2 * GLA_RANK].set(gla_w2[l, 1])
        y_b = _gla(main, small, w2.astype(BF16), gla_b[l].reshape(2, 1, QK_W).astype(F32),
                   gla_norm[l].reshape(1, V_W).astype(F32))

        x2d = _outproj(y_a, y_b, x2d, w_out[l].astype(BF16), norm_mix_post[l].reshape(1, D_MODEL))

        h = _ffn_up(x2d, norm_ffn_pre[l].reshape(1, D_MODEL), ffn_w_gate[l].astype(BF16), ffn_w_up[l].astype(BF16),
                    ffn_conv_w[l].astype(F32), ffn_conv_b[l].reshape(1, D_FF).astype(F32))
        x2d = _ffn_down(h, x2d, ffn_w_down[l].astype(BF16), norm_ffn_post[l].reshape(1, D_MODEL))
    return x2d.reshape(BATCH, SEQ, D_MODEL)
```

```python
import functools

import jax
import jax.numpy as jnp
from jax import lax
from jax.experimental import pallas as pl
from jax.experimental.pallas import tpu as pltpu

F32 = jnp.float32
BF16 = jnp.bfloat16

D_MODEL = 1024
BATCH = 8
SEQ = 2048
DEPTH = 4
TOKENS = BATCH * SEQ
HEADS = 4
DK = 64
DV = 128
QK_W = HEADS * DK
V_W = HEADS * DV
GLA_RANK = 16
GLA_GATE_TEMP = 16.0
D_FF = 2816
EPS = 1e-6
IN_WIDTHS = (2 * QK_W, V_W, V_W, 4 * HEADS, QK_W, QK_W, V_W, V_W, 2 * GLA_RANK)
N_MAIN = 2 * QK_W + V_W + V_W + QK_W + QK_W + V_W + V_W
N_SMALL = 4 * HEADS + 2 * GLA_RANK
SMALL_PAD = 128

LANES = 128
CHUNK = 128
NCHUNK = SEQ // CHUNK
ROW_BLK = 256
TM = 512
FF_TN = 256
FF_RB = 512
FF_ACT_RB = 128
GLA_LEVELS = (64, 32, 16, 8, 4, 2)
GLA_MILD_LOG_DECAY = 60.0
VMEM_LIMIT = 56 * 1024 * 1024


def _dot(a, b):
    return jnp.dot(a, b, preferred_element_type=F32)


def _dot_nt(a, b):
    return lax.dot_general(a, b, (((1,), (1,)), ((), ())), preferred_element_type=F32)


def _dot_tn(a, b):
    return lax.dot_general(a, b, (((0,), (0,)), ((), ())), preferred_element_type=F32)


def _split_bf16(x):
    hi = x.astype(BF16)
    r1 = x - hi.astype(F32)
    mid = r1.astype(BF16)
    lo = (r1 - mid.astype(F32)).astype(BF16)
    return hi, mid, lo


def _sel_dot(sel, x):
    hi, mid, lo = _split_bf16(x)
    return _dot(sel, hi) + _dot(sel, mid) + _dot(sel, lo)


def _dot_sel(x, sel):
    hi, mid, lo = _split_bf16(x)
    return _dot(hi, sel) + _dot(mid, sel) + _dot(lo, sel)


def _log_sigmoid(x):
    return jnp.minimum(x, 0.0) - jnp.log1p(jnp.exp(-jnp.abs(x)))


def _sigmoid(x):
    return 1.0 / (1.0 + jnp.exp(-x))


def _rms_scale(x):
    return lax.rsqrt(jnp.mean(x * x, axis=-1, keepdims=True) + EPS)


def _rmsnorm_kernel(x_ref, g_ref, o_ref):
    x = x_ref[...]
    o_ref[...] = (x * _rms_scale(x) * g_ref[...]).astype(BF16)


def _rmsnorm(x2d, g):
    return pl.pallas_call(
        _rmsnorm_kernel,
        grid=(TOKENS // TM,),
        in_specs=[
            pl.BlockSpec((TM, D_MODEL), lambda i: (i, 0)),
            pl.BlockSpec((1, D_MODEL), lambda i: (0, 0)),
        ],
        out_specs=pl.BlockSpec((TM, D_MODEL), lambda i: (i, 0)),
        out_shape=jax.ShapeDtypeStruct((TOKENS, D_MODEL), BF16),
        compiler_params=pltpu.CompilerParams(
            dimension_semantics=("arbitrary",), vmem_limit_bytes=VMEM_LIMIT),
        name="rmsnorm",
    )(x2d, g)


def _inproj_kernel(hn_ref, wm_ref, ws_ref, wst_ref, main_ref, small_ref, smallt_ref):
    hn = hn_ref[...]
    for j in range(N_MAIN // 512):
        cols = slice(j * 512, (j + 1) * 512)
        main_ref[:, cols] = _dot(hn, wm_ref[:, cols]).astype(BF16)
    small_ref[...] = _dot(hn, ws_ref[...])
    st = _dot_nt(wst_ref[...], hn)
    for j in range(TM // CHUNK):
        smallt_ref[j] = st[:, j * CHUNK:(j + 1) * CHUNK]


def _inproj(hn, w_main, w_small, w_small_t):
    return pl.pallas_call(
        _inproj_kernel,
        grid=(TOKENS // TM,),
        in_specs=[
            pl.BlockSpec((TM, D_MODEL), lambda i: (i, 0)),
            pl.BlockSpec((D_MODEL, N_MAIN), lambda i: (0, 0)),
            pl.BlockSpec((D_MODEL, SMALL_PAD), lambda i: (0, 0)),
            pl.BlockSpec((N_SMALL, D_MODEL), lambda i: (0, 0)),
        ],
        out_specs=[
            pl.BlockSpec((TM, N_MAIN), lambda i: (i, 0)),
            pl.BlockSpec((TM, SMALL_PAD), lambda i: (i, 0)),
            pl.BlockSpec((TM // CHUNK, N_SMALL, CHUNK), lambda i: (i, 0, 0)),
        ],
        out_shape=[
            jax.ShapeDtypeStruct((TOKENS, N_MAIN), BF16),
            jax.ShapeDtypeStruct((TOKENS, SMALL_PAD), F32),
            jax.ShapeDtypeStruct((TOKENS // CHUNK, N_SMALL, CHUNK), F32),
        ],
        compiler_params=pltpu.CompilerParams(
            dimension_semantics=("arbitrary",), vmem_limit_bytes=VMEM_LIMIT),
        name="inproj",
    )(hn, w_main, w_small, w_small_t)


def _tri_masks():
    t = lax.broadcasted_iota(jnp.int32, (CHUNK, CHUNK), 0)
    s = lax.broadcasted_iota(jnp.int32, (CHUNK, CHUNK), 1)
    return t, s


def _blk(idx, m):
    return jnp.right_shift(idx, m.bit_length() - 1)


def _half_mask(hh):
    lane = lax.broadcasted_iota(jnp.int32, (CHUNK, LANES), 1)
    return (lane < DK) if hh == 0 else (lane >= DK)


def _head_norm_gate_pass(sum_sc, gate_ref, nw_ref, y_ref, gate_fn):
    def body(r, carry):
        rows = pl.ds(pl.multiple_of(r * ROW_BLK, ROW_BLK), ROW_BLK)
        for h in range(HEADS):
            cols = slice(h * DV, (h + 1) * DV)
            hs = sum_sc[rows, cols]
            yn = hs * _rms_scale(hs) * nw_ref[:, cols]
            gate = gate_ref[rows, cols].astype(F32)
            y_ref[rows, cols] = (gate_fn(gate) * yn).astype(BF16)
        return carry
    lax.fori_loop(0, SEQ // ROW_BLK, body, 0)


def _mlstm_kernel(aqk_ref, av_ref, ao_ref, gcol_ref, grow_ref, gbrow_ref, gbcol_ref, cw_ref, cb_ref, nw_ref,
                  y_ref, q_sc, k_sc, hsum_sc, c_sc):

    def conv_body(r, carry):
        r0 = pl.multiple_of(r * ROW_BLK, ROW_BLK)
        cur = aqk_ref[pl.ds(r0, ROW_BLK), :].astype(F32)
        p0 = pl.multiple_of(jnp.maximum(r0 - 16, 0), 16)
        n0 = pl.multiple_of(jnp.minimum(r0 + ROW_BLK, SEQ - 16), 16)
        prev_row = aqk_ref[pl.ds(p0, 16), :].astype(F32)[15:16]
        next_row = aqk_ref[pl.ds(n0, 16), :].astype(F32)[0:1]
        prev_row = jnp.where(r == 0, 0.0, prev_row)
        next_row = jnp.where(r == SEQ // ROW_BLK - 1, 0.0, next_row)
        row = lax.broadcasted_iota(jnp.int32, (ROW_BLK, 1), 0)
        prev = jnp.where(row == 0, prev_row, pltpu.roll(cur, 1, 0))
        nxt = jnp.where(row == ROW_BLK - 1, next_row, pltpu.roll(cur, ROW_BLK - 1, 0))
        c = prev * cw_ref[0:1, :] + cur * cw_ref[1:2, :] + nxt * cw_ref[2:3, :] + cb_ref[...]
        act = c * _sigmoid(c)
        q_sc[pl.ds(r0, ROW_BLK), :] = act[:, :QK_W].astype(BF16)
        k_sc[pl.ds(r0, ROW_BLK), :] = (act[:, QK_W:] * (DK ** -0.5)).astype(BF16)
        return carry
    lax.fori_loop(0, SEQ // ROW_BLK, conv_body, 0)

    hsum_sc[...] = jnp.zeros_like(hsum_sc)
    c_sc[...] = jnp.zeros_like(c_sc)

    t_idx, s_idx = _tri_masks()
    le = t_idx >= s_idx
    ge = t_idx <= s_idx
    tri_le = le.astype(BF16)
    tri_ge = ge.astype(BF16)
    ones_v = jnp.ones((CHUNK, DV), BF16)

    def one_dir(c, direction, m_st):
        r0 = pl.multiple_of(c * CHUNK, CHUNK)
        rows = pl.ds(r0, CHUNK)
        mask = le if direction == 0 else ge
        t_last = CHUNK - 1 if direction == 0 else 0
        i_off = 0 if direction == 0 else 2 * HEADS
        f_off = i_off + HEADS

        g_row = grow_ref[c][0:4 * HEADS, :] + gbrow_ref[...]
        lf_row = _log_sigmoid(g_row)
        b_row_all = _dot_sel(lf_row, tri_ge if direction == 0 else tri_le)
        g_col = gcol_ref[rows, :] + gbcol_ref[...]
        lf_col = _log_sigmoid(g_col)
        b_col_all = _sel_dot(tri_le if direction == 0 else tri_ge, lf_col)

        new_m = []
        for p in range(HEADS // 2):
            lanes = slice(p * LANES, (p + 1) * LANES)
            q2 = q_sc[rows, lanes]
            k2 = k_sc[rows, lanes]
            k2f = k2.astype(F32)
            q2f = q2.astype(F32)
            c_pair = c_sc[direction, p]
            c_pair_bf = c_pair.astype(BF16)
            upd_pair = []
            decay_rows = []
            for hh in range(2):
                h = 2 * p + hh
                half = _half_mask(hh)
                m_h = m_st[direction * HEADS + h]
                a_row = g_row[i_off + h:i_off + h + 1, :] - b_row_all[f_off + h:f_off + h + 1, :]
                b_col = b_col_all[:, f_off + h:f_off + h + 1]
                a_col = g_col[:, i_off + h:i_off + h + 1] - b_col
                a_mat = jnp.broadcast_to(a_row, (CHUNK, CHUNK))
                pmax = jnp.max(jnp.where(mask, a_mat, -jnp.inf), axis=1, keepdims=True)
                u = jnp.maximum(m_h, pmax)
                e = jnp.where(mask, jnp.exp(a_mat - u), 0.0)
                qm = jnp.where(half, q2, jnp.zeros_like(q2))
                sraw = _dot_nt(qm, k2)
                pm = (sraw * e).astype(BF16)
                inter = jnp.exp(m_h - u)
                qi = jnp.where(half, q2f * inter, 0.0).astype(BF16)
                v_ext = jnp.concatenate([av_ref[rows, h * DV:(h + 1) * DV], ones_v], axis=1)
                nd = _dot(pm, v_ext) + _dot(qi, c_pair_bf)
                num = nd[:, :DV]
                den = nd[:, DV:]
                thr = jnp.exp(-b_col - u)
                hout = num / jnp.maximum(jnp.abs(den), thr)
                hsum_sc[rows, h * DV:(h + 1) * DV] += hout
                u_last = u[t_last:t_last + 1, :]
                b_last = b_col[t_last:t_last + 1, :]
                new_m.append(b_last + u_last)
                decay = jnp.exp(m_h - u_last)
                w_col = jnp.exp(a_col - u_last)
                wk = jnp.where(half, k2f * w_col, 0.0).astype(BF16)
                upd_pair.append(_dot_tn(wk, v_ext))
                decay_rows.append(jnp.broadcast_to(decay, (DK, 2 * DV)))
            decay_mat = jnp.concatenate(decay_rows, axis=0)
            c_sc[direction, p] = decay_mat * c_pair + upd_pair[0] + upd_pair[1]
        return new_m

    def chunk_body(i, m_st):
        m_fw = one_dir(i, 0, m_st)
        m_bw = one_dir(NCHUNK - 1 - i, 1, m_st)
        return tuple(m_fw + m_bw)

    m0 = tuple(jnp.zeros((1, 1), F32) for _ in range(2 * HEADS))
    lax.fori_loop(0, NCHUNK, chunk_body, m0)

    _head_norm_gate_pass(hsum_sc, ao_ref, nw_ref, y_ref, _sigmoid)


def _mlstm(main, small, small_t, gb_row, gb_col, conv_w, conv_b, norm_w):
    def blk(j):
        return pl.BlockSpec((SEQ, V_W), lambda b: (b, j))
    full = lambda shape: pl.BlockSpec(shape, lambda b: tuple(0 for _ in shape))
    return pl.pallas_call(
        _mlstm_kernel,
        grid=(BATCH,),
        in_specs=[
            blk(0), blk(1), blk(2),
            pl.BlockSpec((SEQ, SMALL_PAD), lambda b: (b, 0)),
            pl.BlockSpec((NCHUNK, N_SMALL, CHUNK), lambda b: (b, 0, 0)),
            full((4 * HEADS, CHUNK)), full((1, SMALL_PAD)), full((3, 2 * QK_W)), full((1, 2 * QK_W)), full((1, V_W)),
        ],
        out_specs=pl.BlockSpec((SEQ, V_W), lambda b: (b, 0)),
        out_shape=jax.ShapeDtypeStruct((TOKENS, V_W), BF16),
        scratch_shapes=[
            pltpu.VMEM((SEQ, QK_W), BF16),
            pltpu.VMEM((SEQ, QK_W), BF16),
            pltpu.VMEM((SEQ, V_W), F32),
            pltpu.VMEM((2, HEADS // 2, 2 * DK, 2 * DV), F32),
        ],
        compiler_params=pltpu.CompilerParams(
            dimension_semantics=("arbitrary",), vmem_limit_bytes=VMEM_LIMIT),
        name="mlstm",
    )(main, main, main, small, small_t, gb_row, gb_col, conv_w, conv_b, norm_w)


def _gla_kernel(bqk_ref, bv_ref, br_ref, gcol_ref, w2_ref, gb_ref, nw_ref,
                y_ref, q_sc, k_sc, la_sc, osum_sc, st_sc):

    def pre_body(r, tot_min):
        rows = pl.ds(pl.multiple_of(r * ROW_BLK, ROW_BLK), ROW_BLK)
        qk = bqk_ref[rows, :].astype(F32)
        q_sc[rows, :] = (qk[:, :QK_W] * (DK ** -0.5)).astype(BF16)
        k_sc[rows, :] = qk[:, QK_W:].astype(BF16)
        lr = gcol_ref[rows, :].astype(BF16)
        for d in range(2):
            z = _dot(lr, w2_ref[d]) + gb_ref[d]
            la = _log_sigmoid(z) * (1.0 / GLA_GATE_TEMP)
            la_sc[d, rows, :] = la
            for cc in range(ROW_BLK // CHUNK):
                chunk_total = jnp.sum(la[cc * CHUNK:(cc + 1) * CHUNK, :], axis=0, keepdims=True)
                tot_min = jnp.minimum(tot_min, chunk_total)
        return tot_min
    tot_min = lax.fori_loop(0, SEQ // ROW_BLK, pre_body, jnp.zeros((1, QK_W), F32))
    all_mild = jnp.min(tot_min) >= -GLA_MILD_LOG_DECAY

    osum_sc[...] = jnp.zeros_like(osum_sc)
    st_sc[...] = jnp.zeros_like(st_sc)

    t_idx, s_idx = _tri_masks()
    eye = t_idx == s_idx
    lane2 = lax.broadcasted_iota(jnp.int32, (CHUNK, 2 * DV), 1)
    left_half = lane2 < DV
    bd_rows = lax.broadcasted_iota(jnp.int32, (2 * DV, LANES), 0) < DV
    bd_lanes = lax.broadcasted_iota(jnp.int32, (2 * DV, LANES), 1) < DK
    state_mask = bd_rows == bd_lanes

    def intra_mild(q2, k2, b_p, causal):
        qf = (q2 * jnp.exp(b_p)).astype(BF16)
        kf = (k2 * jnp.exp(-b_p)).astype(BF16)
        kb = jnp.concatenate([jnp.where(_half_mask(0), kf, jnp.zeros_like(kf)),
                              jnp.where(_half_mask(1), kf, jnp.zeros_like(kf))], axis=0)
        a_pair = _dot_nt(qf, kb)
        causal2 = jnp.concatenate([causal, causal], axis=1)
        return jnp.where(causal2, a_pair, 0.0)

    def intra_robust(q2, k2, la_p, seg_sum_p, direction, causal, strict_rev):
        k2b = k2.astype(BF16)
        acc = []
        for hh in range(2):
            qm = jnp.where(_half_mask(hh), q2, 0.0).astype(BF16)
            acc.append(jnp.where(eye, _dot_nt(qm, k2b), 0.0))
        for m in GLA_LEVELS + (1,):
            if m == 1:
                dq, dk = la_p, None
            else:
                same = _blk(t_idx, m) == _blk(s_idx, m)
                dq = seg_sum_p(same & causal)
                dk = seg_sum_p(same & strict_rev)
            qt = q2 * jnp.exp(dq)
            ktb = (k2 if dk is None else k2 * jnp.exp(dk)).astype(BF16)
            tb = _blk(t_idx, m)
            sb = _blk(s_idx, m)
            if direction == 0:
                lvl_mask = ((tb & 1) == 1) & (sb == tb - 1)
            else:
                lvl_mask = ((tb & 1) == 0) & (sb == tb + 1)
            for hh in range(2):
                qm = jnp.where(_half_mask(hh), qt, 0.0).astype(BF16)
                acc[hh] = acc[hh] + jnp.where(lvl_mask, _dot_nt(qm, ktb), 0.0)
        return jnp.concatenate(acc, axis=1)

    def one_dir(c, direction, mild):
        r0 = pl.multiple_of(c * CHUNK, CHUNK)
        rows = pl.ds(r0, CHUNK)
        causal = (t_idx >= s_idx) if direction == 0 else (t_idx <= s_idx)
        strict_rev = (s_idx > t_idx) if direction == 0 else (s_idx < t_idx)
        t_last = CHUNK - 1 if direction == 0 else 0

        la = la_sc[direction, rows, :]
        b = _sel_dot(causal.astype(BF16), la)
        b_tot = b[t_last:t_last + 1, :]
        eb = jnp.exp(b)
        k_decay = jnp.exp(b_tot - b)
        state_decay = jnp.exp(b_tot)

        for p in range(HEADS // 2):
            lanes = slice(p * LANES, (p + 1) * LANES)
            q2 = q_sc[rows, lanes].astype(F32)
            k2 = k_sc[rows, lanes].astype(F32)
            if mild:
                a_pair = intra_mild(q2, k2, b[:, lanes], causal)
            else:
                la_p3 = _split_bf16(la[:, lanes])

                def seg_sum_p(sel):
                    selb = sel.astype(BF16)
                    return _dot(selb, la_p3[0]) + _dot(selb, la_p3[1]) + _dot(selb, la_p3[2])
                a_pair = intra_robust(q2, k2, la[:, lanes], seg_sum_p, direction, causal, strict_rev)
            v_pair = bv_ref[rows, p * 2 * DV:(p + 1) * 2 * DV]
            zeros_v = jnp.zeros_like(v_pair)
            v_bd = jnp.concatenate([jnp.where(left_half, v_pair, zeros_v),
                                    jnp.where(left_half, zeros_v, v_pair)], axis=0)
            st = st_sc[direction, p]
            q_inter = (q2 * eb[:, lanes]).astype(BF16)
            o = _dot(a_pair.astype(BF16), v_bd) + _dot_nt(q_inter, st.astype(BF16))
            osum_sc[rows, p * 2 * DV:(p + 1) * 2 * DV] += o
            k_state = (k2 * k_decay[:, lanes]).astype(BF16)
            upd_t = _dot_tn(v_pair, k_state)
            st_sc[direction, p] = st * state_decay[:, lanes] + jnp.where(state_mask, upd_t, 0.0)

    def run(mild):
        def chunk_body(i, carry):
            one_dir(i, 0, mild)
            one_dir(NCHUNK - 1 - i, 1, mild)
            return carry
        lax.fori_loop(0, NCHUNK, chunk_body, 0)

    @pl.when(all_mild)
    def _():
        run(True)

    @pl.when(jnp.logical_not(all_mild))
    def _():
        run(False)

    _head_norm_gate_pass(osum_sc, br_ref, nw_ref, y_ref, lambda g: g * _sigmoid(g))


def _gla(main, small, w2, gb, norm_w):
    full = lambda shape: pl.BlockSpec(shape, lambda b: tuple(0 for _ in shape))
    return pl.pallas_call(
        _gla_kernel,
        grid=(BATCH,),
        in_specs=[
            pl.BlockSpec((SEQ, V_W), lambda b: (b, 3)),
            pl.BlockSpec((SEQ, V_W), lambda b: (b, 4)),
            pl.BlockSpec((SEQ, V_W), lambda b: (b, 5)),
            pl.BlockSpec((SEQ, SMALL_PAD), lambda b: (b, 0)),
            full((2, SMALL_PAD, QK_W)), full((2, 1, QK_W)), full((1, V_W)),
        ],
        out_specs=pl.BlockSpec((SEQ, V_W), lambda b: (b, 0)),
        out_shape=jax.ShapeDtypeStruct((TOKENS, V_W), BF16),
        scratch_shapes=[
            pltpu.VMEM((SEQ, QK_W), BF16),
            pltpu.VMEM((SEQ, QK_W), BF16),
            pltpu.VMEM((2, SEQ, QK_W), F32),
            pltpu.VMEM((SEQ, V_W), F32),
            pltpu.VMEM((2, HEADS // 2, 2 * DV, 2 * DK), F32),
        ],
        compiler_params=pltpu.CompilerParams(
            dimension_semantics=("arbitrary",), vmem_limit_bytes=VMEM_LIMIT),
        name="gla",
    )(main, main, main, small, w2, gb, norm_w)


def _residual_and_next_norm(x_ref, branch, g_post_ref, g_next_ref, o_ref, hn_ref):
    x_new = x_ref[...] + branch * _rms_scale(branch) * g_post_ref[...]
    o_ref[...] = x_new
    if hn_ref is not None:
        hn_ref[...] = (x_new * _rms_scale(x_new) * g_next_ref[...]).astype(BF16)


def _outproj_kernel(ya_ref, yb_ref, x_ref, w_ref, g_post_ref, g_next_ref, o_ref, hn_ref):
    mix = _dot(ya_ref[...], w_ref[:V_W, :]) + _dot(yb_ref[...], w_ref[V_W:, :])
    _residual_and_next_norm(x_ref, mix, g_post_ref, g_next_ref, o_ref, hn_ref)


def _row_tile(width):
    return pl.BlockSpec((TM, width), lambda i: (i, 0))


def _resident(shape):
    return pl.BlockSpec(shape, lambda i: (0, 0))


def _outproj(ya, yb, x2d, w, g_post, g_next):
    return pl.pallas_call(
        _outproj_kernel,
        grid=(TOKENS // TM,),
        in_specs=[_row_tile(V_W), _row_tile(V_W), _row_tile(D_MODEL), _resident((D_MODEL, D_MODEL)),
                  _resident((1, D_MODEL)), _resident((1, D_MODEL))],
        out_specs=[_row_tile(D_MODEL), _row_tile(D_MODEL)],
        out_shape=[jax.ShapeDtypeStruct((TOKENS, D_MODEL), F32),
                   jax.ShapeDtypeStruct((TOKENS, D_MODEL), BF16)],
        compiler_params=pltpu.CompilerParams(
            dimension_semantics=("arbitrary",), vmem_limit_bytes=VMEM_LIMIT),
        name="outproj",
    )(ya, yb, x2d, w, g_post, g_next)


def _gelu_tanh(x):
    c = 0.7978845608028654
    half_x = 0.5 * x
    return half_x + half_x * jnp.tanh(x * (c + (c * 0.044715) * (x * x)))


def _ffn_up_kernel(hn_ref, wg_ref, wu_ref, cw_ref, cb_ref, h_ref, g_sc, u_sc):
    g_sc[0:8, :] = jnp.zeros((8, FF_TN), F32)
    g_sc[SEQ + 8:SEQ + 16, :] = jnp.zeros((8, FF_TN), F32)
    n_blk = SEQ // FF_RB

    def project(i):
        rows = slice(i * FF_RB, (i + 1) * FF_RB)
        hn = hn_ref[rows, :]
        g_sc[8 + i * FF_RB:8 + (i + 1) * FF_RB, :] = _dot(hn, wg_ref[...])
        u_sc[rows, :] = _dot(hn, wu_ref[...])

    row = lax.broadcasted_iota(jnp.int32, (FF_ACT_RB, 1), 0)
    first_row = row == 0
    last_row = row == FF_ACT_RB - 1

    def activate(i):
        for j in range(FF_RB // FF_ACT_RB):
            r0 = i * FF_RB + j * FF_ACT_RB
            cur = g_sc[r0 + 8:r0 + 8 + FF_ACT_RB, :]
            prev_row = g_sc[r0 + 7:r0 + 8, :]
            next_row = g_sc[r0 + 8 + FF_ACT_RB:r0 + 9 + FF_ACT_RB, :]
            prev = jnp.where(first_row, prev_row, pltpu.roll(cur, 1, 0))
            nxt = jnp.where(last_row, next_row, pltpu.roll(cur, FF_ACT_RB - 1, 0))
            gc = prev * cw_ref[0:1, :] + cur * cw_ref[1:2, :] + nxt * cw_ref[2:3, :] + cb_ref[...]
            h_ref[r0:r0 + FF_ACT_RB, :] = (_gelu_tanh(gc) * u_sc[r0:r0 + FF_ACT_RB, :]).astype(BF16)

    for i in range(n_blk + 1):
        if i < n_blk:
            project(i)
        if i >= 1:
            activate(i - 1)


def _ffn_up(hn, wg, wu, cw, cb):
    return pl.pallas_call(
        _ffn_up_kernel,
        grid=(BATCH, D_FF // FF_TN),
        in_specs=[
            pl.BlockSpec((SEQ, D_MODEL), lambda b, n: (b, 0)),
            pl.BlockSpec((D_MODEL, FF_TN), lambda b, n: (0, n)),
            pl.BlockSpec((D_MODEL, FF_TN), lambda b, n: (0, n)),
            pl.BlockSpec((3, FF_TN), lambda b, n: (0, n)),
            pl.BlockSpec((1, FF_TN), lambda b, n: (0, n)),
        ],
        out_specs=pl.BlockSpec((SEQ, FF_TN), lambda b, n: (b, n)),
        out_shape=jax.ShapeDtypeStruct((TOKENS, D_FF), BF16),
        scratch_shapes=[
            pltpu.VMEM((SEQ + 16, FF_TN), F32),
            pltpu.VMEM((SEQ, FF_TN), F32),
        ],
        compiler_params=pltpu.CompilerParams(
            dimension_semantics=("arbitrary", "arbitrary"), vmem_limit_bytes=VMEM_LIMIT),
        name="ffn_up",
    )(hn, wg, wu, cw, cb)


def _ffn_down_kernel(h_ref, x_ref, w_ref, g_post_ref, g_next_ref, o_ref, hn_ref):
    ff = _dot(h_ref[...], w_ref[...])
    _residual_and_next_norm(x_ref, ff, g_post_ref, g_next_ref, o_ref, hn_ref)


def _ffn_down_last_kernel(h_ref, x_ref, w_ref, g_post_ref, o_ref):
    ff = _dot(h_ref[...], w_ref[...])
    _residual_and_next_norm(x_ref, ff, g_post_ref, None, o_ref, None)


def _ffn_down(h, x2d, w, g_post, g_next):
    last = g_next is None
    in_specs = [_row_tile(D_FF), _row_tile(D_MODEL), _resident((D_FF, D_MODEL)), _resident((1, D_MODEL))]
    out_specs = [_row_tile(D_MODEL)]
    out_shape = [jax.ShapeDtypeStruct((TOKENS, D_MODEL), F32)]
    args = [h, x2d, w, g_post]
    if not last:
        in_specs.append(_resident((1, D_MODEL)))
        out_specs.append(_row_tile(D_MODEL))
        out_shape.append(jax.ShapeDtypeStruct((TOKENS, D_MODEL), BF16))
        args.append(g_next)
    outs = pl.pallas_call(
        _ffn_down_last_kernel if last else _ffn_down_kernel,
        grid=(TOKENS // TM,),
        in_specs=in_specs,
        out_specs=out_specs,
        out_shape=out_shape,
        compiler_params=pltpu.CompilerParams(
            dimension_semantics=("arbitrary",), vmem_limit_bytes=VMEM_LIMIT),
        name="ffn_down_last" if last else "ffn_down",
    )(*args)
    return (outs[0], None) if last else (outs[0], outs[1])


def _split_in_weights(w_in_l):
    offs = [0]
    for w in IN_WIDTHS:
        offs.append(offs[-1] + w)
    a_qk, a_v, a_o, gates, b_q, b_k, b_v, b_r, b_lr = (w_in_l[:, offs[i]:offs[i + 1]] for i in range(9))
    w_main = jnp.concatenate([a_qk, a_v, a_o, b_q, b_k, b_v, b_r], axis=1).astype(BF16)
    small = jnp.concatenate([gates, b_lr], axis=1)
    w_small = jnp.pad(small, ((0, 0), (0, SMALL_PAD - N_SMALL))).astype(BF16)
    w_small_t = small.T.astype(BF16)
    return w_main, w_small, w_small_t


def kernel(x, norm_mix_pre, norm_mix_post, norm_ffn_pre, norm_ffn_post, w_in, mlstm_gate_b, mlstm_conv_w,
           mlstm_conv_b, mlstm_norm, gla_w2, gla_b, gla_norm, w_out, ffn_w_gate, ffn_w_up, ffn_conv_w,
           ffn_conv_b, ffn_w_down):
    def row(v):
        return v.reshape(1, -1).astype(F32)

    x2d = x.reshape(TOKENS, D_MODEL).astype(F32)
    hn = _rmsnorm(x2d, row(norm_mix_pre[0]))
    for l in range(DEPTH):
        w_main, w_small, w_small_t = _split_in_weights(w_in[l])
        main, small, small_t = _inproj(hn, w_main, w_small, w_small_t)

        gate_b = mlstm_gate_b[l].astype(F32)
        gb_row = jnp.broadcast_to(gate_b[:, None], (4 * HEADS, CHUNK))
        gb_col = jnp.pad(gate_b, (0, SMALL_PAD - 4 * HEADS)).reshape(1, SMALL_PAD)
        y_a = _mlstm(main, small, small_t, gb_row, gb_col,
                     mlstm_conv_w[l].astype(F32), row(mlstm_conv_b[l]), row(mlstm_norm[l]))

        w2 = jnp.zeros((2, SMALL_PAD, QK_W), F32)
        w2 = w2.at[0, 4 * HEADS:4 * HEADS + GLA_RANK].set(gla_w2[l, 0])
        w2 = w2.at[1, 4 * HEADS + GLA_RANK:4 * HEADS + 2 * GLA_RANK].set(gla_w2[l, 1])
        y_b = _gla(main, small, w2.astype(BF16), gla_b[l].reshape(2, 1, QK_W).astype(F32), row(gla_norm[l]))

        x2d, hn = _outproj(y_a, y_b, x2d, w_out[l].astype(BF16), row(norm_mix_post[l]), row(norm_ffn_pre[l]))

        h = _ffn_up(hn, ffn_w_gate[l].astype(BF16), ffn_w_up[l].astype(BF16),
                    ffn_conv_w[l].astype(F32), row(ffn_conv_b[l]))
        g_next = row(norm_mix_pre[l + 1]) if l + 1 < DEPTH else None
        x2d, hn = _ffn_down(h, x2d, ffn_w_down[l].astype(BF16), row(norm_ffn_post[l]), g_next)
    return x2d.reshape(BATCH, SEQ, D_MODEL)
```

```python
import functools

import jax
import jax.numpy as jnp
from jax import lax
from jax.experimental import pallas as pl
from jax.experimental.pallas import tpu as pltpu

F32 = jnp.float32
BF16 = jnp.bfloat16

D_MODEL = 1024
BATCH = 8
SEQ = 2048
DEPTH = 4
TOKENS = BATCH * SEQ
HEADS = 4
DK = 64
DV = 128
QK_W = HEADS * DK
V_W = HEADS * DV
GLA_RANK = 16
GLA_GATE_TEMP = 16.0
D_FF = 2816
EPS = 1e-6
IN_WIDTHS = (2 * QK_W, V_W, V_W, 4 * HEADS, QK_W, QK_W, V_W, V_W, 2 * GLA_RANK)
N_MAIN = 2 * QK_W + V_W + V_W + QK_W + QK_W + V_W + V_W
N_GATES = 4 * HEADS
SMALL_W = 256
LR_OFF = 16

LANES = 128
CHUNK = 128
NCHUNK = SEQ // CHUNK
ROW_BLK = 256
TM = 1024
FF_TN = 256
FF_RB = 512
FF_ACT_RB = 128
GLA_LEVELS = (64, 32, 16, 8, 4, 2)
GLA_MILD_LOG_DECAY = 60.0
VMEM_LIMIT = 56 * 1024 * 1024


def _dot(a, b):
    return jnp.dot(a, b, preferred_element_type=F32)


def _dot_nt(a, b):
    return lax.dot_general(a, b, (((1,), (1,)), ((), ())), preferred_element_type=F32)


def _dot_tn(a, b):
    return lax.dot_general(a, b, (((0,), (0,)), ((), ())), preferred_element_type=F32)


def _scan(x, axis, reverse, op, identity):
    n = x.shape[axis]
    idx = lax.broadcasted_iota(jnp.int32, x.shape, axis)
    k = 1
    while k < n:
        if reverse:
            shifted = jnp.where(idx < n - k, pltpu.roll(x, n - k, axis), identity)
        else:
            shifted = jnp.where(idx >= k, pltpu.roll(x, k, axis), identity)
        x = op(x, shifted)
        k *= 2
    return x


def _split_bf16(x):
    hi = x.astype(BF16)
    r1 = x - hi.astype(F32)
    mid = r1.astype(BF16)
    lo = (r1 - mid.astype(F32)).astype(BF16)
    return hi, mid, lo


def _sel_dot(sel, x):
    hi, mid, lo = _split_bf16(x)
    return _dot(sel, hi) + _dot(sel, mid) + _dot(sel, lo)


def _dot_sel(x, sel):
    hi, mid, lo = _split_bf16(x)
    return _dot(hi, sel) + _dot(mid, sel) + _dot(lo, sel)


def _log_sigmoid(x):
    return jnp.minimum(x, 0.0) - jnp.log1p(jnp.exp(-jnp.abs(x)))


def _sigmoid(x):
    return 1.0 / (1.0 + jnp.exp(-x))


def _rms_scale(x):
    return lax.rsqrt(jnp.mean(x * x, axis=-1, keepdims=True) + EPS)


def _rmsnorm_kernel(x_ref, g_ref, o_ref):
    x = x_ref[...]
    o_ref[...] = (x * _rms_scale(x) * g_ref[...]).astype(BF16)


def _rmsnorm(x2d, g):
    return pl.pallas_call(
        _rmsnorm_kernel,
        grid=(TOKENS // TM,),
        in_specs=[
            pl.BlockSpec((TM, D_MODEL), lambda i: (i, 0)),
            pl.BlockSpec((1, D_MODEL), lambda i: (0, 0)),
        ],
        out_specs=pl.BlockSpec((TM, D_MODEL), lambda i: (i, 0)),
        out_shape=jax.ShapeDtypeStruct((TOKENS, D_MODEL), BF16),
        compiler_params=pltpu.CompilerParams(
            dimension_semantics=("arbitrary",), vmem_limit_bytes=VMEM_LIMIT),
        name="rmsnorm",
    )(x2d, g)


def _inproj_kernel(hn_ref, wm_ref, ws_ref, wst_ref, main_ref, small_ref, smallt_ref):
    hn = hn_ref[...]
    for j in range(N_MAIN // 512):
        cols = slice(j * 512, (j + 1) * 512)
        main_ref[:, cols] = _dot(hn, wm_ref[:, cols]).astype(BF16)
    small_ref[...] = _dot(hn, ws_ref[...])
    st = _dot_nt(wst_ref[...], hn)
    for j in range(TM // CHUNK):
        smallt_ref[j] = st[:, j * CHUNK:(j + 1) * CHUNK]


def _inproj(hn, w_main, w_small, w_small_t):
    return pl.pallas_call(
        _inproj_kernel,
        grid=(TOKENS // TM,),
        in_specs=[
            pl.BlockSpec((TM, D_MODEL), lambda i: (i, 0)),
            pl.BlockSpec((D_MODEL, N_MAIN), lambda i: (0, 0)),
            pl.BlockSpec((D_MODEL, SMALL_W), lambda i: (0, 0)),
            pl.BlockSpec((N_GATES, D_MODEL), lambda i: (0, 0)),
        ],
        out_specs=[
            pl.BlockSpec((TM, N_MAIN), lambda i: (i, 0)),
            pl.BlockSpec((TM, SMALL_W), lambda i: (i, 0)),
            pl.BlockSpec((TM // CHUNK, N_GATES, CHUNK), lambda i: (i, 0, 0)),
        ],
        out_shape=[
            jax.ShapeDtypeStruct((TOKENS, N_MAIN), BF16),
            jax.ShapeDtypeStruct((TOKENS, SMALL_W), F32),
            jax.ShapeDtypeStruct((TOKENS // CHUNK, N_GATES, CHUNK), F32),
        ],
        compiler_params=pltpu.CompilerParams(
            dimension_semantics=("arbitrary",), vmem_limit_bytes=VMEM_LIMIT),
        name="inproj",
    )(hn, w_main, w_small, w_small_t)


def _tri_masks():
    t = lax.broadcasted_iota(jnp.int32, (CHUNK, CHUNK), 0)
    s = lax.broadcasted_iota(jnp.int32, (CHUNK, CHUNK), 1)
    return t, s


def _blk(idx, m):
    return jnp.right_shift(idx, m.bit_length() - 1)


def _half_mask(hh):
    lane = lax.broadcasted_iota(jnp.int32, (CHUNK, LANES), 1)
    return (lane < DK) if hh == 0 else (lane >= DK)


def _head_norm_gate_pass(sum_sc, gate_ref, nw_ref, y_ref, gate_fn):
    def body(r, carry):
        rows = pl.ds(pl.multiple_of(r * ROW_BLK, ROW_BLK), ROW_BLK)
        for h in range(HEADS):
            cols = slice(h * DV, (h + 1) * DV)
            hs = sum_sc[rows, cols]
            yn = hs * _rms_scale(hs) * nw_ref[:, cols]
            gate = gate_ref[rows, cols].astype(F32)
            y_ref[rows, cols] = (gate_fn(gate) * yn).astype(BF16)
        return carry
    lax.fori_loop(0, SEQ // ROW_BLK, body, 0)


def _mlstm_kernel(aqk_ref, av_ref, ao_ref, gcol_ref, grow_ref, gbrow_ref, gbcol_ref, cw_ref, cb_ref, nw_ref,
                  y_ref, q_sc, k_sc, hsum_sc, ct_sc):

    def conv_body(r, carry):
        r0 = pl.multiple_of(r * ROW_BLK, ROW_BLK)
        cur = aqk_ref[pl.ds(r0, ROW_BLK), :].astype(F32)
        p0 = pl.multiple_of(jnp.maximum(r0 - 16, 0), 16)
        n0 = pl.multiple_of(jnp.minimum(r0 + ROW_BLK, SEQ - 16), 16)
        prev_row = aqk_ref[pl.ds(p0, 16), :].astype(F32)[15:16]
        next_row = aqk_ref[pl.ds(n0, 16), :].astype(F32)[0:1]
        prev_row = jnp.where(r == 0, 0.0, prev_row)
        next_row = jnp.where(r == SEQ // ROW_BLK - 1, 0.0, next_row)
        row = lax.broadcasted_iota(jnp.int32, (ROW_BLK, 1), 0)
        prev = jnp.where(row == 0, prev_row, pltpu.roll(cur, 1, 0))
        nxt = jnp.where(row == ROW_BLK - 1, next_row, pltpu.roll(cur, ROW_BLK - 1, 0))
        c = prev * cw_ref[0:1, :] + cur * cw_ref[1:2, :] + nxt * cw_ref[2:3, :] + cb_ref[...]
        act = c * _sigmoid(c)
        q_sc[pl.ds(r0, ROW_BLK), :] = act[:, :QK_W].astype(BF16)
        k_sc[pl.ds(r0, ROW_BLK), :] = (act[:, QK_W:] * (DK ** -0.5)).astype(BF16)
        return carry
    lax.fori_loop(0, SEQ // ROW_BLK, conv_body, 0)

    hsum_sc[...] = jnp.zeros_like(hsum_sc)
    ct_sc[...] = jnp.zeros_like(ct_sc)

    t_idx, s_idx = _tri_masks()
    le = t_idx >= s_idx
    ge = t_idx <= s_idx
    tri_le = le.astype(BF16)
    tri_ge = ge.astype(BF16)
    lane_row = lax.broadcasted_iota(jnp.int32, (1, LANES), 1)
    half0 = _half_mask(0)
    half0_row = lane_row < DK
    lane2 = lax.broadcasted_iota(jnp.int32, (CHUNK, 2 * DV), 1)
    left_half = lane2 < DV
    bd_r = lax.broadcasted_iota(jnp.int32, (2 * DV, 2 * DV), 0) < DV
    bd_l = lax.broadcasted_iota(jnp.int32, (2 * DV, 2 * DV), 1) < DV
    ones_bd = (bd_r == bd_l).astype(BF16)
    st_r = lax.broadcasted_iota(jnp.int32, (2 * DV, LANES), 0) < DV
    st_l = lax.broadcasted_iota(jnp.int32, (2 * DV, LANES), 1) < DK
    state_mask = st_r == st_l

    def lane_bc(x, j):
        return jnp.broadcast_to(x[:, j:j + 1], (x.shape[0], LANES))

    def one_dir(c, direction, m_vec, n_rows):
        r0 = pl.multiple_of(c * CHUNK, CHUNK)
        rows = pl.ds(r0, CHUNK)
        mask = le if direction == 0 else ge
        t_last = CHUNK - 1 if direction == 0 else 0

        g_row = grow_ref[c] + gbrow_ref[...]
        lf_row = _log_sigmoid(g_row[2 * HEADS:, :])
        b_row = _dot_sel(lf_row, tri_ge if direction == 0 else tri_le)
        a_row = g_row[:2 * HEADS, :] - b_row
        g_col = gcol_ref[rows, :] + gbcol_ref[...]
        lf_col = _log_sigmoid(g_col[:, LANES:])
        b_col = _sel_dot(tri_le if direction == 0 else tri_ge, lf_col)
        a_col = g_col[:, :LANES] - b_col
        pmax = _scan(a_col, 0, direction == 1, jnp.maximum, -jnp.inf)
        u = jnp.maximum(m_vec, pmax)
        u_last = u[t_last:t_last + 1, :]
        m_new = b_col[t_last:t_last + 1, :] + u_last
        thr = jnp.exp(-b_col - u)
        w = jnp.exp(a_col - u_last)
        decay = jnp.exp(m_vec - u_last)

        n_new = []
        for p in range(HEADS // 2):
            lanes = slice(p * LANES, (p + 1) * LANES)
            hd = [direction * HEADS + 2 * p, direction * HEADS + 2 * p + 1]
            q2 = q_sc[rows, lanes]
            k2 = k_sc[rows, lanes]
            u_bc = [lane_bc(u, j) for j in hd]
            e_pair = jnp.concatenate(
                [jnp.where(mask, jnp.exp(jnp.broadcast_to(a_row[j:j + 1, :], (CHUNK, CHUNK)) - ub), 0.0)
                 for j, ub in zip(hd, u_bc)], axis=1)
            zeros_k = jnp.zeros_like(k2)
            kb = jnp.concatenate([jnp.where(half0, k2, zeros_k), jnp.where(half0, zeros_k, k2)], axis=0)
            pm = (_dot_nt(q2, kb) * e_pair).astype(BF16)
            v_pair = av_ref[rows, p * 2 * DV:(p + 1) * 2 * DV]
            zeros_v = jnp.zeros_like(v_pair)
            v_bd = jnp.concatenate([jnp.where(left_half, v_pair, zeros_v),
                                    jnp.where(left_half, zeros_v, v_pair)], axis=0)
            inter = jnp.where(half0, jnp.exp(lane_bc(m_vec, hd[0]) - u_bc[0]),
                              jnp.exp(lane_bc(m_vec, hd[1]) - u_bc[1]))
            qi = (q2.astype(F32) * inter).astype(BF16)
            ct = ct_sc[direction, p]
            n_row = n_rows[direction * (HEADS // 2) + p]
            n_bd = jnp.where(state_mask, jnp.broadcast_to(n_row, (2 * DV, LANES)), 0.0).astype(BF16)
            num = _dot(pm, v_bd) + _dot_nt(qi, ct.astype(BF16))
            den = _dot(pm, ones_bd) + _dot_nt(qi, n_bd)
            thr_pair = jnp.concatenate([lane_bc(thr, hd[0]), lane_bc(thr, hd[1])], axis=1)
            hsum_sc[rows, p * 2 * DV:(p + 1) * 2 * DV] += num / jnp.maximum(jnp.abs(den), thr_pair)
            w_pair = jnp.where(half0, lane_bc(w, hd[0]), lane_bc(w, hd[1]))
            wk = k2.astype(F32) * w_pair
            decay_pair = jnp.where(half0_row, lane_bc(decay, hd[0]), lane_bc(decay, hd[1]))
            upd_t = _dot_tn(v_pair, wk.astype(BF16))
            ct_sc[direction, p] = ct * decay_pair + jnp.where(state_mask, upd_t, 0.0)
            n_new.append(n_row * decay_pair + jnp.sum(wk, axis=0, keepdims=True))
        return m_new, n_new

    def chunk_body(i, carry):
        m_vec, n_rows = carry[0], carry[1:]
        m_fw, n_fw = one_dir(i, 0, m_vec, n_rows)
        m_bw, n_bw = one_dir(NCHUNK - 1 - i, 1, m_vec, n_rows)
        m_next = jnp.where(lane_row < HEADS, m_fw, m_bw)
        return (m_next,) + tuple(n_fw) + tuple(n_bw)

    init = tuple(jnp.zeros((1, LANES), F32) for _ in range(1 + 2 * (HEADS // 2)))
    lax.fori_loop(0, NCHUNK, chunk_body, init)

    _head_norm_gate_pass(hsum_sc, ao_ref, nw_ref, y_ref, _sigmoid)


def _mlstm(main, small, small_t, gb_row, gb_col, conv_w, conv_b, norm_w):
    def blk(j):
        return pl.BlockSpec((SEQ, V_W), lambda b: (b, j))
    full = lambda shape: pl.BlockSpec(shape, lambda b: tuple(0 for _ in shape))
    return pl.pallas_call(
        _mlstm_kernel,
        grid=(BATCH,),
        in_specs=[
            blk(0), blk(1), blk(2),
            pl.BlockSpec((SEQ, SMALL_W), lambda b: (b, 0)),
            pl.BlockSpec((NCHUNK, N_GATES, CHUNK), lambda b: (b, 0, 0)),
            full((N_GATES, CHUNK)), full((1, SMALL_W)), full((3, 2 * QK_W)), full((1, 2 * QK_W)), full((1, V_W)),
        ],
        out_specs=pl.BlockSpec((SEQ, V_W), lambda b: (b, 0)),
        out_shape=jax.ShapeDtypeStruct((TOKENS, V_W), BF16),
        scratch_shapes=[
            pltpu.VMEM((SEQ, QK_W), BF16),
            pltpu.VMEM((SEQ, QK_W), BF16),
            pltpu.VMEM((SEQ, V_W), F32),
            pltpu.VMEM((2, HEADS // 2, 2 * DV, 2 * DK), F32),
        ],
        compiler_params=pltpu.CompilerParams(
            dimension_semantics=("arbitrary",), vmem_limit_bytes=VMEM_LIMIT),
        name="mlstm",
    )(main, main, main, small, small_t, gb_row, gb_col, conv_w, conv_b, norm_w)


def _gla_kernel(bqk_ref, bv_ref, br_ref, gcol_ref, w2_ref, gb_ref, nw_ref,
                y_ref, q_sc, k_sc, la_sc, osum_sc, st_sc):

    def pre_body(r, tot_min):
        rows = pl.ds(pl.multiple_of(r * ROW_BLK, ROW_BLK), ROW_BLK)
        qk = bqk_ref[rows, :].astype(F32)
        q_sc[rows, :] = (qk[:, :QK_W] * (DK ** -0.5)).astype(BF16)
        k_sc[rows, :] = qk[:, QK_W:].astype(BF16)
        lr = gcol_ref[rows, :LANES].astype(BF16)
        for d in range(2):
            z = _dot(lr, w2_ref[d]) + gb_ref[d]
            la = _log_sigmoid(z) * (1.0 / GLA_GATE_TEMP)
            la_sc[d, rows, :] = la
            for cc in range(ROW_BLK // CHUNK):
                chunk_total = jnp.sum(la[cc * CHUNK:(cc + 1) * CHUNK, :], axis=0, keepdims=True)
                tot_min = jnp.minimum(tot_min, chunk_total)
        return tot_min
    tot_min = lax.fori_loop(0, SEQ // ROW_BLK, pre_body, jnp.zeros((1, QK_W), F32))
    all_mild = jnp.min(tot_min) >= -GLA_MILD_LOG_DECAY

    osum_sc[...] = jnp.zeros_like(osum_sc)
    st_sc[...] = jnp.zeros_like(st_sc)

    t_idx, s_idx = _tri_masks()
    eye = t_idx == s_idx
    lane2 = lax.broadcasted_iota(jnp.int32, (CHUNK, 2 * DV), 1)
    left_half = lane2 < DV
    bd_rows = lax.broadcasted_iota(jnp.int32, (2 * DV, LANES), 0) < DV
    bd_lanes = lax.broadcasted_iota(jnp.int32, (2 * DV, LANES), 1) < DK
    state_mask = bd_rows == bd_lanes

    def intra_mild(q2, k2, b_p, causal):
        qf = (q2 * jnp.exp(b_p)).astype(BF16)
        kf = (k2 * jnp.exp(-b_p)).astype(BF16)
        kb = jnp.concatenate([jnp.where(_half_mask(0), kf, jnp.zeros_like(kf)),
                              jnp.where(_half_mask(1), kf, jnp.zeros_like(kf))], axis=0)
        a_pair = _dot_nt(qf, kb)
        causal2 = jnp.concatenate([causal, causal], axis=1)
        return jnp.where(causal2, a_pair, 0.0)

    def intra_robust(q2, k2, la_p, seg_sum_p, direction, causal, strict_rev):
        k2b = k2.astype(BF16)
        acc = []
        for hh in range(2):
            qm = jnp.where(_half_mask(hh), q2, 0.0).astype(BF16)
            acc.append(jnp.where(eye, _dot_nt(qm, k2b), 0.0))
        for m in GLA_LEVELS + (1,):
            if m == 1:
                dq, dk = la_p, None
            else:
                same = _blk(t_idx, m) == _blk(s_idx, m)
                dq = seg_sum_p(same & causal)
                dk = seg_sum_p(same & strict_rev)
            qt = q2 * jnp.exp(dq)
            ktb = (k2 if dk is None else k2 * jnp.exp(dk)).astype(BF16)
            tb = _blk(t_idx, m)
            sb = _blk(s_idx, m)
            if direction == 0:
                lvl_mask = ((tb & 1) == 1) & (sb == tb - 1)
            else:
                lvl_mask = ((tb & 1) == 0) & (sb == tb + 1)
            for hh in range(2):
                qm = jnp.where(_half_mask(hh), qt, 0.0).astype(BF16)
                acc[hh] = acc[hh] + jnp.where(lvl_mask, _dot_nt(qm, ktb), 0.0)
        return jnp.concatenate(acc, axis=1)

    def one_dir(c, direction, mild):
        r0 = pl.multiple_of(c * CHUNK, CHUNK)
        rows = pl.ds(r0, CHUNK)
        causal = (t_idx >= s_idx) if direction == 0 else (t_idx <= s_idx)
        strict_rev = (s_idx > t_idx) if direction == 0 else (s_idx < t_idx)
        t_last = CHUNK - 1 if direction == 0 else 0

        la = la_sc[direction, rows, :]
        b = _scan(la, 0, direction == 1, jnp.add, 0.0)
        b_tot = b[t_last:t_last + 1, :]
        eb = jnp.exp(b)
        k_decay = jnp.exp(b_tot - b)
        state_decay = jnp.exp(b_tot)

        for p in range(HEADS // 2):
            lanes = slice(p * LANES, (p + 1) * LANES)
            q2 = q_sc[rows, lanes].astype(F32)
            k2 = k_sc[rows, lanes].astype(F32)
            if mild:
                a_pair = intra_mild(q2, k2, b[:, lanes], causal)
            else:
                la_p3 = _split_bf16(la[:, lanes])

                def seg_sum_p(sel):
                    selb = sel.astype(BF16)
                    return _dot(selb, la_p3[0]) + _dot(selb, la_p3[1]) + _dot(selb, la_p3[2])
                a_pair = intra_robust(q2, k2, la[:, lanes], seg_sum_p, direction, causal, strict_rev)
            v_pair = bv_ref[rows, p * 2 * DV:(p + 1) * 2 * DV]
            zeros_v = jnp.zeros_like(v_pair)
            v_bd = jnp.concatenate([jnp.where(left_half, v_pair, zeros_v),
                                    jnp.where(left_half, zeros_v, v_pair)], axis=0)
            st = st_sc[direction, p]
            q_inter = (q2 * eb[:, lanes]).astype(BF16)
            o = _dot(a_pair.astype(BF16), v_bd) + _dot_nt(q_inter, st.astype(BF16))
            osum_sc[rows, p * 2 * DV:(p + 1) * 2 * DV] += o
            k_state = (k2 * k_decay[:, lanes]).astype(BF16)
            upd_t = _dot_tn(v_pair, k_state)
            st_sc[direction, p] = st * state_decay[:, lanes] + jnp.where(state_mask, upd_t, 0.0)

    def run(mild):
        def chunk_body(i, carry):
            one_dir(i, 0, mild)
            one_dir(NCHUNK - 1 - i, 1, mild)
            return carry
        lax.fori_loop(0, NCHUNK, chunk_body, 0, unroll=2 if mild else 1)

    @pl.when(all_mild)
    def _():
        run(True)

    @pl.when(jnp.logical_not(all_mild))
    def _():
        run(False)

    _head_norm_gate_pass(osum_sc, br_ref, nw_ref, y_ref, lambda g: g * _sigmoid(g))


def _gla(main, small, w2, gb, norm_w):
    full = lambda shape: pl.BlockSpec(shape, lambda b: tuple(0 for _ in shape))
    return pl.pallas_call(
        _gla_kernel,
        grid=(BATCH,),
        in_specs=[
            pl.BlockSpec((SEQ, V_W), lambda b: (b, 3)),
            pl.BlockSpec((SEQ, V_W), lambda b: (b, 4)),
            pl.BlockSpec((SEQ, V_W), lambda b: (b, 5)),
            pl.BlockSpec((SEQ, SMALL_W), lambda b: (b, 0)),
            full((2, LANES, QK_W)), full((2, 1, QK_W)), full((1, V_W)),
        ],
        out_specs=pl.BlockSpec((SEQ, V_W), lambda b: (b, 0)),
        out_shape=jax.ShapeDtypeStruct((TOKENS, V_W), BF16),
        scratch_shapes=[
            pltpu.VMEM((SEQ, QK_W), BF16),
            pltpu.VMEM((SEQ, QK_W), BF16),
            pltpu.VMEM((2, SEQ, QK_W), F32),
            pltpu.VMEM((SEQ, V_W), F32),
            pltpu.VMEM((2, HEADS // 2, 2 * DV, 2 * DK), F32),
        ],
        compiler_params=pltpu.CompilerParams(
            dimension_semantics=("arbitrary",), vmem_limit_bytes=VMEM_LIMIT),
        name="gla",
    )(main, main, main, small, w2, gb, norm_w)


def _residual_and_next_norm(x_ref, branch, g_post_ref, g_next_ref, o_ref, hn_ref):
    x_new = x_ref[...] + branch * _rms_scale(branch) * g_post_ref[...]
    o_ref[...] = x_new
    if hn_ref is not None:
        hn_ref[...] = (x_new * _rms_scale(x_new) * g_next_ref[...]).astype(BF16)


def _outproj_kernel(ya_ref, yb_ref, x_ref, w_ref, g_post_ref, g_next_ref, o_ref, hn_ref):
    mix = _dot(ya_ref[...], w_ref[:V_W, :]) + _dot(yb_ref[...], w_ref[V_W:, :])
    _residual_and_next_norm(x_ref, mix, g_post_ref, g_next_ref, o_ref, hn_ref)


def _row_tile(width):
    return pl.BlockSpec((TM, width), lambda i: (i, 0))


def _resident(shape):
    return pl.BlockSpec(shape, lambda i: (0, 0))


def _outproj(ya, yb, x2d, w, g_post, g_next):
    return pl.pallas_call(
        _outproj_kernel,
        grid=(TOKENS // TM,),
        in_specs=[_row_tile(V_W), _row_tile(V_W), _row_tile(D_MODEL), _resident((D_MODEL, D_MODEL)),
                  _resident((1, D_MODEL)), _resident((1, D_MODEL))],
        out_specs=[_row_tile(D_MODEL), _row_tile(D_MODEL)],
        out_shape=[jax.ShapeDtypeStruct((TOKENS, D_MODEL), F32),
                   jax.ShapeDtypeStruct((TOKENS, D_MODEL), BF16)],
        compiler_params=pltpu.CompilerParams(
            dimension_semantics=("arbitrary",), vmem_limit_bytes=VMEM_LIMIT),
        name="outproj",
    )(ya, yb, x2d, w, g_post, g_next)


def _gelu_tanh(x):
    c = 0.7978845608028654
    half_x = 0.5 * x
    return half_x + half_x * jnp.tanh(x * (c + (c * 0.044715) * (x * x)))


def _ffn_up_kernel(hn_ref, wg_ref, wu_ref, cw_ref, cb_ref, h_ref, g_sc, u_sc):
    g_sc[0:8, :] = jnp.zeros((8, FF_TN), F32)
    g_sc[SEQ + 8:SEQ + 16, :] = jnp.zeros((8, FF_TN), F32)
    n_blk = SEQ // FF_RB

    def project(i):
        rows = slice(i * FF_RB, (i + 1) * FF_RB)
        hn = hn_ref[rows, :]
        g_sc[8 + i * FF_RB:8 + (i + 1) * FF_RB, :] = _dot(hn, wg_ref[...])
        u_sc[rows, :] = _dot(hn, wu_ref[...])

    row = lax.broadcasted_iota(jnp.int32, (FF_ACT_RB, 1), 0)
    first_row = row == 0
    last_row = row == FF_ACT_RB - 1

    def activate(i):
        for j in range(FF_RB // FF_ACT_RB):
            r0 = i * FF_RB + j * FF_ACT_RB
            cur = g_sc[r0 + 8:r0 + 8 + FF_ACT_RB, :]
            prev_row = g_sc[r0 + 7:r0 + 8, :]
            next_row = g_sc[r0 + 8 + FF_ACT_RB:r0 + 9 + FF_ACT_RB, :]
            prev = jnp.where(first_row, prev_row, pltpu.roll(cur, 1, 0))
            nxt = jnp.where(last_row, next_row, pltpu.roll(cur, FF_ACT_RB - 1, 0))
            gc = prev * cw_ref[0:1, :] + cur * cw_ref[1:2, :] + nxt * cw_ref[2:3, :] + cb_ref[...]
            h_ref[r0:r0 + FF_ACT_RB, :] = (_gelu_tanh(gc) * u_sc[r0:r0 + FF_ACT_RB, :]).astype(BF16)

    for i in range(n_blk + 1):
        if i < n_blk:
            project(i)
        if i >= 1:
            activate(i - 1)


def _ffn_up(hn, wg, wu, cw, cb):
    return pl.pallas_call(
        _ffn_up_kernel,
        grid=(BATCH, D_FF // FF_TN),
        in_specs=[
            pl.BlockSpec((SEQ, D_MODEL), lambda b, n: (b, 0)),
            pl.BlockSpec((D_MODEL, FF_TN), lambda b, n: (0, n)),
            pl.BlockSpec((D_MODEL, FF_TN), lambda b, n: (0, n)),
            pl.BlockSpec((3, FF_TN), lambda b, n: (0, n)),
            pl.BlockSpec((1, FF_TN), lambda b, n: (0, n)),
        ],
        out_specs=pl.BlockSpec((SEQ, FF_TN), lambda b, n: (b, n)),
        out_shape=jax.ShapeDtypeStruct((TOKENS, D_FF), BF16),
        scratch_shapes=[
            pltpu.VMEM((SEQ + 16, FF_TN), F32),
            pltpu.VMEM((SEQ, FF_TN), F32),
        ],
        compiler_params=pltpu.CompilerParams(
            dimension_semantics=("arbitrary", "arbitrary"), vmem_limit_bytes=VMEM_LIMIT),
        name="ffn_up",
    )(hn, wg, wu, cw, cb)


def _ffn_down_kernel(h_ref, x_ref, w_ref, g_post_ref, g_next_ref, o_ref, hn_ref):
    ff = _dot(h_ref[...], w_ref[...])
    _residual_and_next_norm(x_ref, ff, g_post_ref, g_next_ref, o_ref, hn_ref)


def _ffn_down_last_kernel(h_ref, x_ref, w_ref, g_post_ref, o_ref):
    ff = _dot(h_ref[...], w_ref[...])
    _residual_and_next_norm(x_ref, ff, g_post_ref, None, o_ref, None)


def _ffn_down(h, x2d, w, g_post, g_next):
    last = g_next is None
    in_specs = [_row_tile(D_FF), _row_tile(D_MODEL), _resident((D_FF, D_MODEL)), _resident((1, D_MODEL))]
    out_specs = [_row_tile(D_MODEL)]
    out_shape = [jax.ShapeDtypeStruct((TOKENS, D_MODEL), F32)]
    args = [h, x2d, w, g_post]
    if not last:
        in_specs.append(_resident((1, D_MODEL)))
        out_specs.append(_row_tile(D_MODEL))
        out_shape.append(jax.ShapeDtypeStruct((TOKENS, D_MODEL), BF16))
        args.append(g_next)
    outs = pl.pallas_call(
        _ffn_down_last_kernel if last else _ffn_down_kernel,
        grid=(TOKENS // TM,),
        in_specs=in_specs,
        out_specs=out_specs,
        out_shape=out_shape,
        compiler_params=pltpu.CompilerParams(
            dimension_semantics=("arbitrary",), vmem_limit_bytes=VMEM_LIMIT),
        name="ffn_down_last" if last else "ffn_down",
    )(*args)
    return (outs[0], None) if last else (outs[0], outs[1])


def _split_in_weights(w_in_l):
    offs = [0]
    for w in IN_WIDTHS:
        offs.append(offs[-1] + w)
    a_qk, a_v, a_o, gates, b_q, b_k, b_v, b_r, b_lr = (w_in_l[:, offs[i]:offs[i + 1]] for i in range(9))
    w_main = jnp.concatenate([a_qk, a_v, a_o, b_q, b_k, b_v, b_r], axis=1).astype(BF16)
    gates_i, gates_f = _gates_by_direction_head(gates, axis=1)
    pad = lambda n: jnp.zeros((D_MODEL, n), w_in_l.dtype)
    w_small = jnp.concatenate([gates_i, pad(LR_OFF - 2 * HEADS), b_lr, pad(LANES - LR_OFF - 2 * GLA_RANK),
                               gates_f, pad(SMALL_W - LANES - 2 * HEADS)], axis=1).astype(BF16)
    w_gates_t = jnp.concatenate([gates_i, gates_f], axis=1).T.astype(BF16)
    return w_main, w_small, w_gates_t


def _gates_by_direction_head(g, axis):
    i_fw, f_fw, i_bw, f_bw = jnp.split(g, 4, axis=axis)
    return jnp.concatenate([i_fw, i_bw], axis=axis), jnp.concatenate([f_fw, f_bw], axis=axis)


def kernel(x, norm_mix_pre, norm_mix_post, norm_ffn_pre, norm_ffn_post, w_in, mlstm_gate_b, mlstm_conv_w,
           mlstm_conv_b, mlstm_norm, gla_w2, gla_b, gla_norm, w_out, ffn_w_gate, ffn_w_up, ffn_conv_w,
           ffn_conv_b, ffn_w_down):
    def row(v):
        return v.reshape(1, -1).astype(F32)

    x2d = x.reshape(TOKENS, D_MODEL).astype(F32)
    hn = _rmsnorm(x2d, row(norm_mix_pre[0]))
    for l in range(DEPTH):
        w_main, w_small, w_small_t = _split_in_weights(w_in[l])
        main, small, small_t = _inproj(hn, w_main, w_small, w_small_t)

        bias_i, bias_f = _gates_by_direction_head(mlstm_gate_b[l].astype(F32), axis=0)
        gb_row = jnp.broadcast_to(jnp.concatenate([bias_i, bias_f])[:, None], (N_GATES, CHUNK))
        gb_col = jnp.concatenate([bias_i, jnp.zeros((LANES - 2 * HEADS,), F32),
                                  bias_f, jnp.zeros((SMALL_W - LANES - 2 * HEADS,), F32)]).reshape(1, SMALL_W)
        y_a = _mlstm(main, small, small_t, gb_row, gb_col,
                     mlstm_conv_w[l].astype(F32), row(mlstm_conv_b[l]), row(mlstm_norm[l]))

        w2 = jnp.zeros((2, LANES, QK_W), F32)
        w2 = w2.at[0, LR_OFF:LR_OFF + GLA_RANK].set(gla_w2[l, 0])
        w2 = w2.at[1, LR_OFF + GLA_RANK:LR_OFF + 2 * GLA_RANK].set(gla_w2[l, 1])
        y_b = _gla(main, small, w2.astype(BF16), gla_b[l].reshape(2, 1, QK_W).astype(F32), row(gla_norm[l]))

        x2d, hn = _outproj(y_a, y_b, x2d, w_out[l].astype(BF16), row(norm_mix_post[l]), row(norm_ffn_pre[l]))

        h = _ffn_up(hn, ffn_w_gate[l].astype(BF16), ffn_w_up[l].astype(BF16),
                    ffn_conv_w[l].astype(F32), row(ffn_conv_b[l]))
        g_next = row(norm_mix_pre[l + 1]) if l + 1 < DEPTH else None
        x2d, hn = _ffn_down(h, x2d, ffn_w_down[l].astype(BF16), row(norm_ffn_post[l]), g_next)
    return x2d.reshape(BATCH, SEQ, D_MODEL)
```

```python
import functools

import jax
import jax.numpy as jnp
from jax import lax
from jax.experimental import pallas as pl
from jax.experimental.pallas import tpu as pltpu

F32 = jnp.float32
BF16 = jnp.bfloat16

D_MODEL = 1024
BATCH = 8
SEQ = 2048
DEPTH = 4
TOKENS = BATCH * SEQ
HEADS = 4
DK = 64
DV = 128
QK_W = HEADS * DK
V_W = HEADS * DV
GLA_RANK = 16
GLA_GATE_TEMP = 16.0
D_FF = 2816
EPS = 1e-6
IN_WIDTHS = (2 * QK_W, V_W, V_W, 4 * HEADS, QK_W, QK_W, V_W, V_W, 2 * GLA_RANK)
N_MAIN = 2 * QK_W + V_W + V_W + QK_W + QK_W + V_W + V_W
N_GATES = 4 * HEADS
SMALL_W = 256
LR_OFF = 16

LANES = 128
CHUNK = 128
NCHUNK = SEQ // CHUNK
ROW_BLK = 256
TM = 1024
EPI_RB = 256
FF_TN = 256
FF_RB = 128
FF_ACT_RB = 128
GLA_LEVELS = (64, 32, 16, 8, 4, 2)
GLA_MILD_LOG_DECAY = 60.0
VMEM_LIMIT = 56 * 1024 * 1024


def _dot(a, b):
    return jnp.dot(a, b, preferred_element_type=F32)


def _dot_nt(a, b):
    return lax.dot_general(a, b, (((1,), (1,)), ((), ())), preferred_element_type=F32)


def _dot_tn(a, b):
    return lax.dot_general(a, b, (((0,), (0,)), ((), ())), preferred_element_type=F32)


def _scan(x, axis, reverse, op, identity):
    n = x.shape[axis]
    idx = lax.broadcasted_iota(jnp.int32, x.shape, axis)
    k = 1
    while k < n:
        if reverse:
            shifted = jnp.where(idx < n - k, pltpu.roll(x, n - k, axis), identity)
        else:
            shifted = jnp.where(idx >= k, pltpu.roll(x, k, axis), identity)
        x = op(x, shifted)
        k *= 2
    return x


def _split_bf16(x):
    hi = x.astype(BF16)
    r1 = x - hi.astype(F32)
    mid = r1.astype(BF16)
    lo = (r1 - mid.astype(F32)).astype(BF16)
    return hi, mid, lo


def _sel_dot(sel, x):
    hi, mid, lo = _split_bf16(x)
    return _dot(sel, hi) + _dot(sel, mid) + _dot(sel, lo)


def _dot_sel(x, sel):
    hi, mid, lo = _split_bf16(x)
    return _dot(hi, sel) + _dot(mid, sel) + _dot(lo, sel)


def _log_sigmoid(x):
    return jnp.minimum(x, 0.0) - jnp.log1p(jnp.exp(-jnp.abs(x)))


def _sigmoid(x):
    return 1.0 / (1.0 + jnp.exp(-x))


def _rms_scale(x):
    return lax.rsqrt(jnp.mean(x * x, axis=-1, keepdims=True) + EPS)


def _rmsnorm_kernel(x_ref, g_ref, o_ref):
    x = x_ref[...]
    o_ref[...] = (x * _rms_scale(x) * g_ref[...]).astype(BF16)


def _rmsnorm(x2d, g):
    return pl.pallas_call(
        _rmsnorm_kernel,
        grid=(TOKENS // TM,),
        in_specs=[
            pl.BlockSpec((TM, D_MODEL), lambda i: (i, 0)),
            pl.BlockSpec((1, D_MODEL), lambda i: (0, 0)),
        ],
        out_specs=pl.BlockSpec((TM, D_MODEL), lambda i: (i, 0)),
        out_shape=jax.ShapeDtypeStruct((TOKENS, D_MODEL), BF16),
        compiler_params=pltpu.CompilerParams(
            dimension_semantics=("arbitrary",), vmem_limit_bytes=VMEM_LIMIT),
        name="rmsnorm",
    )(x2d, g)


def _inproj_kernel(hn_ref, wm_ref, ws_ref, wst_ref, main_ref, small_ref, smallt_ref):
    hn = hn_ref[...]
    for j in range(N_MAIN // 512):
        cols = slice(j * 512, (j + 1) * 512)
        main_ref[:, cols] = _dot(hn, wm_ref[:, cols]).astype(BF16)
    small_ref[...] = _dot(hn, ws_ref[...])
    st = _dot_nt(wst_ref[...], hn)
    for j in range(TM // CHUNK):
        smallt_ref[j] = st[:, j * CHUNK:(j + 1) * CHUNK]


def _inproj(hn, w_main, w_small, w_small_t):
    return pl.pallas_call(
        _inproj_kernel,
        grid=(TOKENS // TM,),
        in_specs=[
            pl.BlockSpec((TM, D_MODEL), lambda i: (i, 0)),
            pl.BlockSpec((D_MODEL, N_MAIN), lambda i: (0, 0)),
            pl.BlockSpec((D_MODEL, SMALL_W), lambda i: (0, 0)),
            pl.BlockSpec((N_GATES, D_MODEL), lambda i: (0, 0)),
        ],
        out_specs=[
            pl.BlockSpec((TM, N_MAIN), lambda i: (i, 0)),
            pl.BlockSpec((TM, SMALL_W), lambda i: (i, 0)),
            pl.BlockSpec((TM // CHUNK, N_GATES, CHUNK), lambda i: (i, 0, 0)),
        ],
        out_shape=[
            jax.ShapeDtypeStruct((TOKENS, N_MAIN), BF16),
            jax.ShapeDtypeStruct((TOKENS, SMALL_W), F32),
            jax.ShapeDtypeStruct((TOKENS // CHUNK, N_GATES, CHUNK), F32),
        ],
        compiler_params=pltpu.CompilerParams(
            dimension_semantics=("arbitrary",), vmem_limit_bytes=VMEM_LIMIT),
        name="inproj",
    )(hn, w_main, w_small, w_small_t)


def _tri_masks():
    t = lax.broadcasted_iota(jnp.int32, (CHUNK, CHUNK), 0)
    s = lax.broadcasted_iota(jnp.int32, (CHUNK, CHUNK), 1)
    return t, s


def _blk(idx, m):
    return jnp.right_shift(idx, m.bit_length() - 1)


def _half_mask(hh):
    lane = lax.broadcasted_iota(jnp.int32, (CHUNK, LANES), 1)
    return (lane < DK) if hh == 0 else (lane >= DK)


def _head_norm_gate_pass(sum_sc, gate_ref, nw_ref, y_ref, gate_fn):
    def body(r, carry):
        rows = pl.ds(pl.multiple_of(r * ROW_BLK, ROW_BLK), ROW_BLK)
        for h in range(HEADS):
            cols = slice(h * DV, (h + 1) * DV)
            hs = sum_sc[rows, cols]
            yn = hs * _rms_scale(hs) * nw_ref[:, cols]
            gate = gate_ref[rows, cols].astype(F32)
            y_ref[rows, cols] = (gate_fn(gate) * yn).astype(BF16)
        return carry
    lax.fori_loop(0, SEQ // ROW_BLK, body, 0)


def _mlstm_kernel(aqk_ref, av_ref, ao_ref, gcol_ref, grow_ref, gbrow_ref, gbcol_ref, cw_ref, cb_ref, nw_ref,
                  y_ref, q_sc, k_sc, hsum_sc, ct_sc):

    def conv_body(r, carry):
        r0 = pl.multiple_of(r * ROW_BLK, ROW_BLK)
        cur = aqk_ref[pl.ds(r0, ROW_BLK), :].astype(F32)
        p0 = pl.multiple_of(jnp.maximum(r0 - 16, 0), 16)
        n0 = pl.multiple_of(jnp.minimum(r0 + ROW_BLK, SEQ - 16), 16)
        prev_row = aqk_ref[pl.ds(p0, 16), :].astype(F32)[15:16]
        next_row = aqk_ref[pl.ds(n0, 16), :].astype(F32)[0:1]
        prev_row = jnp.where(r == 0, 0.0, prev_row)
        next_row = jnp.where(r == SEQ // ROW_BLK - 1, 0.0, next_row)
        row = lax.broadcasted_iota(jnp.int32, (ROW_BLK, 1), 0)
        prev = jnp.where(row == 0, prev_row, pltpu.roll(cur, 1, 0))
        nxt = jnp.where(row == ROW_BLK - 1, next_row, pltpu.roll(cur, ROW_BLK - 1, 0))
        c = prev * cw_ref[0:1, :] + cur * cw_ref[1:2, :] + nxt * cw_ref[2:3, :] + cb_ref[...]
        act = c * _sigmoid(c)
        q_sc[pl.ds(r0, ROW_BLK), :] = act[:, :QK_W].astype(BF16)
        k_sc[pl.ds(r0, ROW_BLK), :] = (act[:, QK_W:] * (DK ** -0.5)).astype(BF16)
        return carry
    lax.fori_loop(0, SEQ // ROW_BLK, conv_body, 0)

    hsum_sc[...] = jnp.zeros_like(hsum_sc)
    ct_sc[...] = jnp.zeros_like(ct_sc)

    t_idx, s_idx = _tri_masks()
    le = t_idx >= s_idx
    ge = t_idx <= s_idx
    tri_le = le.astype(BF16)
    tri_ge = ge.astype(BF16)
    lane_row = lax.broadcasted_iota(jnp.int32, (1, LANES), 1)
    half0 = _half_mask(0)
    half0_row = lane_row < DK
    lane2 = lax.broadcasted_iota(jnp.int32, (CHUNK, 2 * DV), 1)
    left_half = lane2 < DV
    bd_r = lax.broadcasted_iota(jnp.int32, (2 * DV, 2 * DV), 0) < DV
    bd_l = lax.broadcasted_iota(jnp.int32, (2 * DV, 2 * DV), 1) < DV
    ones_bd = (bd_r == bd_l).astype(BF16)
    st_r = lax.broadcasted_iota(jnp.int32, (2 * DV, LANES), 0) < DV
    st_l = lax.broadcasted_iota(jnp.int32, (2 * DV, LANES), 1) < DK
    state_mask = st_r == st_l

    def lane_bc(x, j):
        return jnp.broadcast_to(x[:, j:j + 1], (x.shape[0], LANES))

    def one_dir(c, direction, m_vec, n_rows):
        r0 = pl.multiple_of(c * CHUNK, CHUNK)
        rows = pl.ds(r0, CHUNK)
        mask = le if direction == 0 else ge
        t_last = CHUNK - 1 if direction == 0 else 0

        g_row = grow_ref[c] + gbrow_ref[...]
        lf_row = _log_sigmoid(g_row[2 * HEADS:, :])
        b_row = _dot_sel(lf_row, tri_ge if direction == 0 else tri_le)
        a_row = g_row[:2 * HEADS, :] - b_row
        g_col = gcol_ref[rows, :] + gbcol_ref[...]
        lf_col = _log_sigmoid(g_col[:, LANES:])
        b_col = _sel_dot(tri_le if direction == 0 else tri_ge, lf_col)
        a_col = g_col[:, :LANES] - b_col
        pmax = _scan(a_col, 0, direction == 1, jnp.maximum, -jnp.inf)
        u = jnp.maximum(m_vec, pmax)
        u_last = u[t_last:t_last + 1, :]
        m_new = b_col[t_last:t_last + 1, :] + u_last
        thr = jnp.exp(-b_col - u)
        w = jnp.exp(a_col - u_last)
        decay = jnp.exp(m_vec - u_last)

        n_new = []
        for p in range(HEADS // 2):
            lanes = slice(p * LANES, (p + 1) * LANES)
            hd = [direction * HEADS + 2 * p, direction * HEADS + 2 * p + 1]
            q2 = q_sc[rows, lanes]
            k2 = k_sc[rows, lanes]
            u_bc = [lane_bc(u, j) for j in hd]
            e_pair = jnp.concatenate(
                [jnp.where(mask, jnp.exp(jnp.broadcast_to(a_row[j:j + 1, :], (CHUNK, CHUNK)) - ub), 0.0)
                 for j, ub in zip(hd, u_bc)], axis=1)
            zeros_k = jnp.zeros_like(k2)
            kb = jnp.concatenate([jnp.where(half0, k2, zeros_k), jnp.where(half0, zeros_k, k2)], axis=0)
            pm = (_dot_nt(q2, kb) * e_pair).astype(BF16)
            v_pair = av_ref[rows, p * 2 * DV:(p + 1) * 2 * DV]
            zeros_v = jnp.zeros_like(v_pair)
            v_bd = jnp.concatenate([jnp.where(left_half, v_pair, zeros_v),
                                    jnp.where(left_half, zeros_v, v_pair)], axis=0)
            inter = jnp.where(half0, jnp.exp(lane_bc(m_vec, hd[0]) - u_bc[0]),
                              jnp.exp(lane_bc(m_vec, hd[1]) - u_bc[1]))
            qi = (q2.astype(F32) * inter).astype(BF16)
            ct = ct_sc[direction, p]
            n_row = n_rows[direction * (HEADS // 2) + p]
            n_bd = jnp.where(state_mask, jnp.broadcast_to(n_row, (2 * DV, LANES)), 0.0).astype(BF16)
            num = _dot(pm, v_bd) + _dot_nt(qi, ct.astype(BF16))
            den = _dot(pm, ones_bd) + _dot_nt(qi, n_bd)
            thr_pair = jnp.concatenate([lane_bc(thr, hd[0]), lane_bc(thr, hd[1])], axis=1)
            hsum_sc[rows, p * 2 * DV:(p + 1) * 2 * DV] += num / jnp.maximum(jnp.abs(den), thr_pair)
            w_pair = jnp.where(half0, lane_bc(w, hd[0]), lane_bc(w, hd[1]))
            wk = k2.astype(F32) * w_pair
            decay_pair = jnp.where(half0_row, lane_bc(decay, hd[0]), lane_bc(decay, hd[1]))
            upd_t = _dot_tn(v_pair, wk.astype(BF16))
            ct_sc[direction, p] = ct * decay_pair + jnp.where(state_mask, upd_t, 0.0)
            n_new.append(n_row * decay_pair + jnp.sum(wk, axis=0, keepdims=True))
        return m_new, n_new

    def chunk_body(i, carry):
        m_vec, n_rows = carry[0], carry[1:]
        m_fw, n_fw = one_dir(i, 0, m_vec, n_rows)
        m_bw, n_bw = one_dir(NCHUNK - 1 - i, 1, m_vec, n_rows)
        m_next = jnp.where(lane_row < HEADS, m_fw, m_bw)
        return (m_next,) + tuple(n_fw) + tuple(n_bw)

    init = tuple(jnp.zeros((1, LANES), F32) for _ in range(1 + 2 * (HEADS // 2)))
    lax.fori_loop(0, NCHUNK, chunk_body, init)

    _head_norm_gate_pass(hsum_sc, ao_ref, nw_ref, y_ref, _sigmoid)


def _mlstm(main, small, small_t, gb_row, gb_col, conv_w, conv_b, norm_w):
    def blk(j):
        return pl.BlockSpec((SEQ, V_W), lambda b: (b, j))
    full = lambda shape: pl.BlockSpec(shape, lambda b: tuple(0 for _ in shape))
    return pl.pallas_call(
        _mlstm_kernel,
        grid=(BATCH,),
        in_specs=[
            blk(0), blk(1), blk(2),
            pl.BlockSpec((SEQ, SMALL_W), lambda b: (b, 0)),
            pl.BlockSpec((NCHUNK, N_GATES, CHUNK), lambda b: (b, 0, 0)),
            full((N_GATES, CHUNK)), full((1, SMALL_W)), full((3, 2 * QK_W)), full((1, 2 * QK_W)), full((1, V_W)),
        ],
        out_specs=pl.BlockSpec((SEQ, V_W), lambda b: (b, 0)),
        out_shape=jax.ShapeDtypeStruct((TOKENS, V_W), BF16),
        scratch_shapes=[
            pltpu.VMEM((SEQ, QK_W), BF16),
            pltpu.VMEM((SEQ, QK_W), BF16),
            pltpu.VMEM((SEQ, V_W), F32),
            pltpu.VMEM((2, HEADS // 2, 2 * DV, 2 * DK), F32),
        ],
        compiler_params=pltpu.CompilerParams(
            dimension_semantics=("arbitrary",), vmem_limit_bytes=VMEM_LIMIT),
        name="mlstm",
    )(main, main, main, small, small_t, gb_row, gb_col, conv_w, conv_b, norm_w)


def _gla_kernel(bqk_ref, bv_ref, br_ref, gcol_ref, w2_ref, gb_ref, nw_ref,
                y_ref, la_sc, osum_sc, st_sc):

    def pre_body(r, tot_min):
        rows = pl.ds(pl.multiple_of(r * ROW_BLK, ROW_BLK), ROW_BLK)
        lr = gcol_ref[rows, :LANES].astype(BF16)
        for d in range(2):
            z = _dot(lr, w2_ref[d]) + gb_ref[d]
            la = _log_sigmoid(z) * (1.0 / GLA_GATE_TEMP)
            la_sc[d, rows, :] = la
            for cc in range(ROW_BLK // CHUNK):
                chunk_total = jnp.sum(la[cc * CHUNK:(cc + 1) * CHUNK, :], axis=0, keepdims=True)
                tot_min = jnp.minimum(tot_min, chunk_total)
        return tot_min
    tot_min = lax.fori_loop(0, SEQ // ROW_BLK, pre_body, jnp.zeros((1, QK_W), F32))
    all_mild = jnp.min(tot_min) >= -GLA_MILD_LOG_DECAY

    osum_sc[...] = jnp.zeros_like(osum_sc)
    st_sc[...] = jnp.zeros_like(st_sc)

    t_idx, s_idx = _tri_masks()
    eye = t_idx == s_idx
    lane2 = lax.broadcasted_iota(jnp.int32, (CHUNK, 2 * DV), 1)
    left_half = lane2 < DV
    bd_rows = lax.broadcasted_iota(jnp.int32, (2 * DV, LANES), 0) < DV
    bd_lanes = lax.broadcasted_iota(jnp.int32, (2 * DV, LANES), 1) < DK
    state_mask = bd_rows == bd_lanes

    def intra_mild(q2, k2, b_p, causal):
        qf = (q2 * jnp.exp(b_p)).astype(BF16)
        kf = (k2 * jnp.exp(-b_p)).astype(BF16)
        kb = jnp.concatenate([jnp.where(_half_mask(0), kf, jnp.zeros_like(kf)),
                              jnp.where(_half_mask(1), kf, jnp.zeros_like(kf))], axis=0)
        a_pair = _dot_nt(qf, kb)
        causal2 = jnp.concatenate([causal, causal], axis=1)
        return jnp.where(causal2, a_pair, 0.0)

    def intra_robust(q2, k2, la_p, seg_sum_p, direction, causal, strict_rev):
        k2b = k2.astype(BF16)
        acc = []
        for hh in range(2):
            qm = jnp.where(_half_mask(hh), q2, 0.0).astype(BF16)
            acc.append(jnp.where(eye, _dot_nt(qm, k2b), 0.0))
        for m in GLA_LEVELS + (1,):
            if m == 1:
                dq, dk = la_p, None
            else:
                same = _blk(t_idx, m) == _blk(s_idx, m)
                dq = seg_sum_p(same & causal)
                dk = seg_sum_p(same & strict_rev)
            qt = q2 * jnp.exp(dq)
            ktb = (k2 if dk is None else k2 * jnp.exp(dk)).astype(BF16)
            tb = _blk(t_idx, m)
            sb = _blk(s_idx, m)
            if direction == 0:
                lvl_mask = ((tb & 1) == 1) & (sb == tb - 1)
            else:
                lvl_mask = ((tb & 1) == 0) & (sb == tb + 1)
            for hh in range(2):
                qm = jnp.where(_half_mask(hh), qt, 0.0).astype(BF16)
                acc[hh] = acc[hh] + jnp.where(lvl_mask, _dot_nt(qm, ktb), 0.0)
        return jnp.concatenate(acc, axis=1)

    def one_dir(c, direction, mild):
        r0 = pl.multiple_of(c * CHUNK, CHUNK)
        rows = pl.ds(r0, CHUNK)
        causal = (t_idx >= s_idx) if direction == 0 else (t_idx <= s_idx)
        strict_rev = (s_idx > t_idx) if direction == 0 else (s_idx < t_idx)
        t_last = CHUNK - 1 if direction == 0 else 0

        la = la_sc[direction, rows, :]
        b = _scan(la, 0, direction == 1, jnp.add, 0.0)
        b_tot = b[t_last:t_last + 1, :]
        eb = jnp.exp(b)
        k_decay = jnp.exp(b_tot - b)
        state_decay = jnp.exp(b_tot)

        for p in range(HEADS // 2):
            lanes = slice(p * LANES, (p + 1) * LANES)
            q2 = bqk_ref[rows, lanes].astype(F32) * (DK ** -0.5)
            k2 = bqk_ref[rows, QK_W + p * LANES:QK_W + (p + 1) * LANES].astype(F32)
            if mild:
                a_pair = intra_mild(q2, k2, b[:, lanes], causal)
            else:
                la_p3 = _split_bf16(la[:, lanes])

                def seg_sum_p(sel):
                    selb = sel.astype(BF16)
                    return _dot(selb, la_p3[0]) + _dot(selb, la_p3[1]) + _dot(selb, la_p3[2])
                a_pair = intra_robust(q2, k2, la[:, lanes], seg_sum_p, direction, causal, strict_rev)
            v_pair = bv_ref[rows, p * 2 * DV:(p + 1) * 2 * DV]
            zeros_v = jnp.zeros_like(v_pair)
            v_bd = jnp.concatenate([jnp.where(left_half, v_pair, zeros_v),
                                    jnp.where(left_half, zeros_v, v_pair)], axis=0)
            st = st_sc[direction, p]
            q_inter = (q2 * eb[:, lanes]).astype(BF16)
            o = _dot(a_pair.astype(BF16), v_bd) + _dot_nt(q_inter, st.astype(BF16))
            osum_sc[rows, p * 2 * DV:(p + 1) * 2 * DV] += o
            k_state = (k2 * k_decay[:, lanes]).astype(BF16)
            upd_t = _dot_tn(v_pair, k_state)
            st_sc[direction, p] = st * state_decay[:, lanes] + jnp.where(state_mask, upd_t, 0.0)

    def run(mild):
        def chunk_body(i, carry):
            one_dir(i, 0, mild)
            one_dir(NCHUNK - 1 - i, 1, mild)
            return carry
        lax.fori_loop(0, NCHUNK, chunk_body, 0, unroll=2 if mild else 1)

    @pl.when(all_mild)
    def _():
        run(True)

    @pl.when(jnp.logical_not(all_mild))
    def _():
        run(False)

    _head_norm_gate_pass(osum_sc, br_ref, nw_ref, y_ref, lambda g: g * _sigmoid(g))


def _gla(main, small, w2, gb, norm_w):
    full = lambda shape: pl.BlockSpec(shape, lambda b: tuple(0 for _ in shape))
    return pl.pallas_call(
        _gla_kernel,
        grid=(BATCH,),
        in_specs=[
            pl.BlockSpec((SEQ, V_W), lambda b: (b, 3)),
            pl.BlockSpec((SEQ, V_W), lambda b: (b, 4)),
            pl.BlockSpec((SEQ, V_W), lambda b: (b, 5)),
            pl.BlockSpec((SEQ, SMALL_W), lambda b: (b, 0)),
            full((2, LANES, QK_W)), full((2, 1, QK_W)), full((1, V_W)),
        ],
        out_specs=pl.BlockSpec((SEQ, V_W), lambda b: (b, 0)),
        out_shape=jax.ShapeDtypeStruct((TOKENS, V_W), BF16),
        scratch_shapes=[
            pltpu.VMEM((2, SEQ, QK_W), F32),
            pltpu.VMEM((SEQ, V_W), F32),
            pltpu.VMEM((2, HEADS // 2, 2 * DV, 2 * DK), F32),
        ],
        compiler_params=pltpu.CompilerParams(
            dimension_semantics=("arbitrary",), vmem_limit_bytes=VMEM_LIMIT),
        name="gla",
    )(main, main, main, small, w2, gb, norm_w)


def _residual_and_next_norm(rows, x_ref, branch, g_post_ref, g_next_ref, o_ref, hn_ref):
    x_new = x_ref[rows, :] + branch * _rms_scale(branch) * g_post_ref[...]
    o_ref[rows, :] = x_new
    if hn_ref is not None:
        hn_ref[rows, :] = (x_new * _rms_scale(x_new) * g_next_ref[...]).astype(BF16)


def _row_blocks():
    return [slice(r, r + EPI_RB) for r in range(0, TM, EPI_RB)]


def _outproj_kernel(ya_ref, yb_ref, x_ref, w_ref, g_post_ref, g_next_ref, o_ref, hn_ref):
    for rows in _row_blocks():
        mix = _dot(ya_ref[rows, :], w_ref[:V_W, :]) + _dot(yb_ref[rows, :], w_ref[V_W:, :])
        _residual_and_next_norm(rows, x_ref, mix, g_post_ref, g_next_ref, o_ref, hn_ref)


def _row_tile(width):
    return pl.BlockSpec((TM, width), lambda i: (i, 0))


def _resident(shape):
    return pl.BlockSpec(shape, lambda i: (0, 0))


def _outproj(ya, yb, x2d, w, g_post, g_next):
    return pl.pallas_call(
        _outproj_kernel,
        grid=(TOKENS // TM,),
        in_specs=[_row_tile(V_W), _row_tile(V_W), _row_tile(D_MODEL), _resident((D_MODEL, D_MODEL)),
                  _resident((1, D_MODEL)), _resident((1, D_MODEL))],
        out_specs=[_row_tile(D_MODEL), _row_tile(D_MODEL)],
        out_shape=[jax.ShapeDtypeStruct((TOKENS, D_MODEL), F32),
                   jax.ShapeDtypeStruct((TOKENS, D_MODEL), BF16)],
        compiler_params=pltpu.CompilerParams(
            dimension_semantics=("arbitrary",), vmem_limit_bytes=VMEM_LIMIT),
        name="outproj",
    )(ya, yb, x2d, w, g_post, g_next)


def _gelu_tanh(x):
    c = 0.7978845608028654
    half_x = 0.5 * x
    return half_x + half_x * jnp.tanh(x * (c + (c * 0.044715) * (x * x)))


def _ffn_up_kernel(hn_ref, wg_ref, wu_ref, cw_ref, cb_ref, h_ref, g_sc, u_sc):
    g_sc[0:8, :] = jnp.zeros((8, FF_TN), F32)
    g_sc[SEQ + 8:SEQ + 16, :] = jnp.zeros((8, FF_TN), F32)
    n_blk = SEQ // FF_RB

    def project(i):
        rows = slice(i * FF_RB, (i + 1) * FF_RB)
        hn = hn_ref[rows, :]
        g_sc[8 + i * FF_RB:8 + (i + 1) * FF_RB, :] = _dot(hn, wg_ref[...])
        u_sc[rows, :] = _dot(hn, wu_ref[...])

    row = lax.broadcasted_iota(jnp.int32, (FF_ACT_RB, 1), 0)
    first_row = row == 0
    last_row = row == FF_ACT_RB - 1

    def activate(i):
        for j in range(FF_RB // FF_ACT_RB):
            r0 = i * FF_RB + j * FF_ACT_RB
            cur = g_sc[r0 + 8:r0 + 8 + FF_ACT_RB, :]
            prev_row = g_sc[r0 + 7:r0 + 8, :]
            next_row = g_sc[r0 + 8 + FF_ACT_RB:r0 + 9 + FF_ACT_RB, :]
            prev = jnp.where(first_row, prev_row, pltpu.roll(cur, 1, 0))
            nxt = jnp.where(last_row, next_row, pltpu.roll(cur, FF_ACT_RB - 1, 0))
            gc = prev * cw_ref[0:1, :] + cur * cw_ref[1:2, :] + nxt * cw_ref[2:3, :] + cb_ref[...]
            h_ref[r0:r0 + FF_ACT_RB, :] = (_gelu_tanh(gc) * u_sc[r0:r0 + FF_ACT_RB, :]).astype(BF16)

    for i in range(n_blk + 1):
        if i < n_blk:
            project(i)
        if i >= 1:
            activate(i - 1)


def _ffn_up(hn, wg, wu, cw, cb):
    return pl.pallas_call(
        _ffn_up_kernel,
        grid=(BATCH, D_FF // FF_TN),
        in_specs=[
            pl.BlockSpec((SEQ, D_MODEL), lambda b, n: (b, 0)),
            pl.BlockSpec((D_MODEL, FF_TN), lambda b, n: (0, n)),
            pl.BlockSpec((D_MODEL, FF_TN), lambda b, n: (0, n)),
            pl.BlockSpec((3, FF_TN), lambda b, n: (0, n)),
            pl.BlockSpec((1, FF_TN), lambda b, n: (0, n)),
        ],
        out_specs=pl.BlockSpec((SEQ, FF_TN), lambda b, n: (b, n)),
        out_shape=jax.ShapeDtypeStruct((TOKENS, D_FF), BF16),
        scratch_shapes=[
            pltpu.VMEM((SEQ + 16, FF_TN), F32),
            pltpu.VMEM((SEQ, FF_TN), F32),
        ],
        compiler_params=pltpu.CompilerParams(
            dimension_semantics=("arbitrary", "arbitrary"), vmem_limit_bytes=VMEM_LIMIT),
        name="ffn_up",
    )(hn, wg, wu, cw, cb)


def _ffn_down_kernel(h_ref, x_ref, w_ref, g_post_ref, g_next_ref, o_ref, hn_ref):
    for rows in _row_blocks():
        ff = _dot(h_ref[rows, :], w_ref[...])
        _residual_and_next_norm(rows, x_ref, ff, g_post_ref, g_next_ref, o_ref, hn_ref)


def _ffn_down_last_kernel(h_ref, x_ref, w_ref, g_post_ref, o_ref):
    for rows in _row_blocks():
        ff = _dot(h_ref[rows, :], w_ref[...])
        _residual_and_next_norm(rows, x_ref, ff, g_post_ref, None, o_ref, None)


def _ffn_down(h, x2d, w, g_post, g_next):
    last = g_next is None
    in_specs = [_row_tile(D_FF), _row_tile(D_MODEL), _resident((D_FF, D_MODEL)), _resident((1, D_MODEL))]
    out_specs = [_row_tile(D_MODEL)]
    out_shape = [jax.ShapeDtypeStruct((TOKENS, D_MODEL), F32)]
    args = [h, x2d, w, g_post]
    if not last:
        in_specs.append(_resident((1, D_MODEL)))
        out_specs.append(_row_tile(D_MODEL))
        out_shape.append(jax.ShapeDtypeStruct((TOKENS, D_MODEL), BF16))
        args.append(g_next)
    outs = pl.pallas_call(
        _ffn_down_last_kernel if last else _ffn_down_kernel,
        grid=(TOKENS // TM,),
        in_specs=in_specs,
        out_specs=out_specs,
        out_shape=out_shape,
        compiler_params=pltpu.CompilerParams(
            dimension_semantics=("arbitrary",), vmem_limit_bytes=VMEM_LIMIT),
        name="ffn_down_last" if last else "ffn_down",
    )(*args)
    return (outs[0], None) if last else (outs[0], outs[1])


def _split_in_weights(w_in_l):
    offs = [0]
    for w in IN_WIDTHS:
        offs.append(offs[-1] + w)
    a_qk, a_v, a_o, gates, b_q, b_k, b_v, b_r, b_lr = (w_in_l[:, offs[i]:offs[i + 1]] for i in range(9))
    w_main = jnp.concatenate([a_qk, a_v, a_o, b_q, b_k, b_v, b_r], axis=1).astype(BF16)
    gates_i, gates_f = _gates_by_direction_head(gates, axis=1)
    pad = lambda n: jnp.zeros((D_MODEL, n), w_in_l.dtype)
    w_small = jnp.concatenate([gates_i, pad(LR_OFF - 2 * HEADS), b_lr, pad(LANES - LR_OFF - 2 * GLA_RANK),
                               gates_f, pad(SMALL_W - LANES - 2 * HEADS)], axis=1).astype(BF16)
    w_gates_t = jnp.concatenate([gates_i, gates_f], axis=1).T.astype(BF16)
    return w_main, w_small, w_gates_t


def _gates_by_direction_head(g, axis):
    i_fw, f_fw, i_bw, f_bw = jnp.split(g, 4, axis=axis)
    return jnp.concatenate([i_fw, i_bw], axis=axis), jnp.concatenate([f_fw, f_bw], axis=axis)


def kernel(x, norm_mix_pre, norm_mix_post, norm_ffn_pre, norm_ffn_post, w_in, mlstm_gate_b, mlstm_conv_w,
           mlstm_conv_b, mlstm_norm, gla_w2, gla_b, gla_norm, w_out, ffn_w_gate, ffn_w_up, ffn_conv_w,
           ffn_conv_b, ffn_w_down):
    def row(v):
        return v.reshape(1, -1).astype(F32)

    x2d = x.reshape(TOKENS, D_MODEL).astype(F32)
    hn = _rmsnorm(x2d, row(norm_mix_pre[0]))
    for l in range(DEPTH):
        w_main, w_small, w_small_t = _split_in_weights(w_in[l])
        main, small, small_t = _inproj(hn, w_main, w_small, w_small_t)

        bias_i, bias_f = _gates_by_direction_head(mlstm_gate_b[l].astype(F32), axis=0)
        gb_row = jnp.broadcast_to(jnp.concatenate([bias_i, bias_f])[:, None], (N_GATES, CHUNK))
        gb_col = jnp.concatenate([bias_i, jnp.zeros((LANES - 2 * HEADS,), F32),
                                  bias_f, jnp.zeros((SMALL_W - LANES - 2 * HEADS,), F32)]).reshape(1, SMALL_W)
        y_a = _mlstm(main, small, small_t, gb_row, gb_col,
                     mlstm_conv_w[l].astype(F32), row(mlstm_conv_b[l]), row(mlstm_norm[l]))

        w2 = jnp.zeros((2, LANES, QK_W), F32)
        w2 = w2.at[0, LR_OFF:LR_OFF + GLA_RANK].set(gla_w2[l, 0])
        w2 = w2.at[1, LR_OFF + GLA_RANK:LR_OFF + 2 * GLA_RANK].set(gla_w2[l, 1])
        y_b = _gla(main, small, w2.astype(BF16), gla_b[l].reshape(2, 1, QK_W).astype(F32), row(gla_norm[l]))

        x2d, hn = _outproj(y_a, y_b, x2d, w_out[l].astype(BF16), row(norm_mix_post[l]), row(norm_ffn_pre[l]))

        h = _ffn_up(hn, ffn_w_gate[l].astype(BF16), ffn_w_up[l].astype(BF16),
                    ffn_conv_w[l].astype(F32), row(ffn_conv_b[l]))
        g_next = row(norm_mix_pre[l + 1]) if l + 1 < DEPTH else None
        x2d, hn = _ffn_down(h, x2d, ffn_w_down[l].astype(BF16), row(norm_ffn_post[l]), g_next)
    return x2d.reshape(BATCH, SEQ, D_MODEL)
```

```python
import functools

import jax
import jax.numpy as jnp
from jax import lax
from jax.experimental import pallas as pl
from jax.experimental.pallas import tpu as pltpu

F32 = jnp.float32
BF16 = jnp.bfloat16

D_MODEL = 1024
BATCH = 8
SEQ = 2048
DEPTH = 4
TOKENS = BATCH * SEQ
HEADS = 4
DK = 64
DV = 128
QK_W = HEADS * DK
V_W = HEADS * DV
GLA_RANK = 16
GLA_GATE_TEMP = 16.0
D_FF = 2816
EPS = 1e-6
IN_WIDTHS = (2 * QK_W, V_W, V_W, 4 * HEADS, QK_W, QK_W, V_W, V_W, 2 * GLA_RANK)
N_MAIN = 2 * QK_W + V_W + V_W + QK_W + QK_W + V_W + V_W
N_GATES = 4 * HEADS
SMALL_W = 256
LR_OFF = 16

LANES = 128
CHUNK = 128
NCHUNK = SEQ // CHUNK
ROW_BLK = 256
TM = 1024
EPI_RB = 256
FF_TN = 256
FF_BLOCKS = (1024, 768, 256)
assert sum(FF_BLOCKS) == SEQ
FF_ACT_RB = 128
GLA_LEVELS = (64, 32, 16, 8, 4, 2)
GLA_MILD_LOG_DECAY = 60.0
VMEM_LIMIT = 56 * 1024 * 1024


def _dot(a, b):
    return jnp.dot(a, b, preferred_element_type=F32)


def _dot_nt(a, b):
    return lax.dot_general(a, b, (((1,), (1,)), ((), ())), preferred_element_type=F32)


def _dot_tn(a, b):
    return lax.dot_general(a, b, (((0,), (0,)), ((), ())), preferred_element_type=F32)


def _scan(x, axis, reverse, op, identity):
    n = x.shape[axis]
    idx = lax.broadcasted_iota(jnp.int32, x.shape, axis)
    k = 1
    while k < n:
        if reverse:
            shifted = jnp.where(idx < n - k, pltpu.roll(x, n - k, axis), identity)
        else:
            shifted = jnp.where(idx >= k, pltpu.roll(x, k, axis), identity)
        x = op(x, shifted)
        k *= 2
    return x


def _split_bf16(x):
    hi = x.astype(BF16)
    r1 = x - hi.astype(F32)
    mid = r1.astype(BF16)
    lo = (r1 - mid.astype(F32)).astype(BF16)
    return hi, mid, lo


def _sel_dot(sel, x):
    hi, mid, lo = _split_bf16(x)
    return _dot(sel, hi) + _dot(sel, mid) + _dot(sel, lo)


def _dot_sel(x, sel):
    hi, mid, lo = _split_bf16(x)
    return _dot(hi, sel) + _dot(mid, sel) + _dot(lo, sel)


def _log_sigmoid(x):
    return jnp.minimum(x, 0.0) - jnp.log(1.0 + jnp.exp(-jnp.abs(x)))


def _sigmoid(x):
    return 1.0 / (1.0 + jnp.exp(-x))


def _rms_scale(x):
    return lax.rsqrt(jnp.mean(x * x, axis=-1, keepdims=True) + EPS)


def _rmsnorm_kernel(x_ref, g_ref, o_ref):
    x = x_ref[...]
    o_ref[...] = (x * _rms_scale(x) * g_ref[...]).astype(BF16)


def _rmsnorm(x2d, g):
    return pl.pallas_call(
        _rmsnorm_kernel,
        grid=(TOKENS // TM,),
        in_specs=[
            pl.BlockSpec((TM, D_MODEL), lambda i: (i, 0)),
            pl.BlockSpec((1, D_MODEL), lambda i: (0, 0)),
        ],
        out_specs=pl.BlockSpec((TM, D_MODEL), lambda i: (i, 0)),
        out_shape=jax.ShapeDtypeStruct((TOKENS, D_MODEL), BF16),
        compiler_params=pltpu.CompilerParams(
            dimension_semantics=("arbitrary",), vmem_limit_bytes=VMEM_LIMIT),
        name="rmsnorm",
    )(x2d, g)


def _inproj_kernel(hn_ref, wm_ref, ws_ref, wst_ref, main_ref, small_ref, smallt_ref):
    hn = hn_ref[...]
    for j in range(N_MAIN // 512):
        cols = slice(j * 512, (j + 1) * 512)
        main_ref[:, cols] = _dot(hn, wm_ref[:, cols]).astype(BF16)
    small_ref[...] = _dot(hn, ws_ref[...])
    st = _dot_nt(wst_ref[...], hn)
    for j in range(TM // CHUNK):
        smallt_ref[j] = st[:, j * CHUNK:(j + 1) * CHUNK]


def _inproj(hn, w_main, w_small, w_small_t):
    return pl.pallas_call(
        _inproj_kernel,
        grid=(TOKENS // TM,),
        in_specs=[
            pl.BlockSpec((TM, D_MODEL), lambda i: (i, 0)),
            pl.BlockSpec((D_MODEL, N_MAIN), lambda i: (0, 0)),
            pl.BlockSpec((D_MODEL, SMALL_W), lambda i: (0, 0)),
            pl.BlockSpec((N_GATES, D_MODEL), lambda i: (0, 0)),
        ],
        out_specs=[
            pl.BlockSpec((TM, N_MAIN), lambda i: (i, 0)),
            pl.BlockSpec((TM, SMALL_W), lambda i: (i, 0)),
            pl.BlockSpec((TM // CHUNK, N_GATES, CHUNK), lambda i: (i, 0, 0)),
        ],
        out_shape=[
            jax.ShapeDtypeStruct((TOKENS, N_MAIN), BF16),
            jax.ShapeDtypeStruct((TOKENS, SMALL_W), F32),
            jax.ShapeDtypeStruct((TOKENS // CHUNK, N_GATES, CHUNK), F32),
        ],
        compiler_params=pltpu.CompilerParams(
            dimension_semantics=("arbitrary",), vmem_limit_bytes=VMEM_LIMIT),
        name="inproj",
    )(hn, w_main, w_small, w_small_t)


def _tri_masks():
    t = lax.broadcasted_iota(jnp.int32, (CHUNK, CHUNK), 0)
    s = lax.broadcasted_iota(jnp.int32, (CHUNK, CHUNK), 1)
    return t, s


def _blk(idx, m):
    return jnp.right_shift(idx, m.bit_length() - 1)


def _half_mask(hh):
    lane = lax.broadcasted_iota(jnp.int32, (CHUNK, LANES), 1)
    return (lane < DK) if hh == 0 else (lane >= DK)


def _head_norm_gate_pass(sum_sc, gate_ref, nw_ref, y_ref, gate_fn):
    def body(r, carry):
        rows = pl.ds(pl.multiple_of(r * ROW_BLK, ROW_BLK), ROW_BLK)
        for h in range(HEADS):
            cols = slice(h * DV, (h + 1) * DV)
            hs = sum_sc[rows, cols]
            yn = hs * _rms_scale(hs) * nw_ref[:, cols]
            gate = gate_ref[rows, cols].astype(F32)
            y_ref[rows, cols] = (gate_fn(gate) * yn).astype(BF16)
        return carry
    lax.fori_loop(0, SEQ // ROW_BLK, body, 0)


def _mlstm_kernel(aqk_ref, av_ref, ao_ref, gcol_ref, grow_ref, gbrow_ref, gbcol_ref, cw_ref, cb_ref, nw_ref,
                  y_ref, q_sc, k_sc, hsum_sc, ct_sc):

    def conv_body(r, carry):
        r0 = pl.multiple_of(r * ROW_BLK, ROW_BLK)
        cur = aqk_ref[pl.ds(r0, ROW_BLK), :].astype(F32)
        p0 = pl.multiple_of(jnp.maximum(r0 - 16, 0), 16)
        n0 = pl.multiple_of(jnp.minimum(r0 + ROW_BLK, SEQ - 16), 16)
        prev_row = aqk_ref[pl.ds(p0, 16), :].astype(F32)[15:16]
        next_row = aqk_ref[pl.ds(n0, 16), :].astype(F32)[0:1]
        prev_row = jnp.where(r == 0, 0.0, prev_row)
        next_row = jnp.where(r == SEQ // ROW_BLK - 1, 0.0, next_row)
        row = lax.broadcasted_iota(jnp.int32, (ROW_BLK, 1), 0)
        prev = jnp.where(row == 0, prev_row, pltpu.roll(cur, 1, 0))
        nxt = jnp.where(row == ROW_BLK - 1, next_row, pltpu.roll(cur, ROW_BLK - 1, 0))
        c = prev * cw_ref[0:1, :] + cur * cw_ref[1:2, :] + nxt * cw_ref[2:3, :] + cb_ref[...]
        act = c * _sigmoid(c)
        q_sc[pl.ds(r0, ROW_BLK), :] = act[:, :QK_W].astype(BF16)
        k_sc[pl.ds(r0, ROW_BLK), :] = (act[:, QK_W:] * (DK ** -0.5)).astype(BF16)
        return carry
    lax.fori_loop(0, SEQ // ROW_BLK, conv_body, 0)

    hsum_sc[...] = jnp.zeros_like(hsum_sc)
    ct_sc[...] = jnp.zeros_like(ct_sc)

    t_idx, s_idx = _tri_masks()
    le = t_idx >= s_idx
    ge = t_idx <= s_idx
    tri_le = le.astype(BF16)
    tri_ge = ge.astype(BF16)
    lane_row = lax.broadcasted_iota(jnp.int32, (1, LANES), 1)
    half0 = _half_mask(0)
    half0_row = lane_row < DK
    lane2 = lax.broadcasted_iota(jnp.int32, (CHUNK, 2 * DV), 1)
    left_half = lane2 < DV
    bd_r = lax.broadcasted_iota(jnp.int32, (2 * DV, 2 * DV), 0) < DV
    bd_l = lax.broadcasted_iota(jnp.int32, (2 * DV, 2 * DV), 1) < DV
    ones_bd = (bd_r == bd_l).astype(BF16)
    st_r = lax.broadcasted_iota(jnp.int32, (2 * DV, LANES), 0) < DV
    st_l = lax.broadcasted_iota(jnp.int32, (2 * DV, LANES), 1) < DK
    state_mask = st_r == st_l

    def lane_bc(x, j):
        return jnp.broadcast_to(x[:, j:j + 1], (x.shape[0], LANES))

    def one_dir(c, direction, m_vec, n_rows):
        r0 = pl.multiple_of(c * CHUNK, CHUNK)
        rows = pl.ds(r0, CHUNK)
        mask = le if direction == 0 else ge
        t_last = CHUNK - 1 if direction == 0 else 0

        g_row = grow_ref[c] + gbrow_ref[...]
        lf_row = _log_sigmoid(g_row[2 * HEADS:, :])
        b_row = _dot_sel(lf_row, tri_ge if direction == 0 else tri_le)
        a_row = g_row[:2 * HEADS, :] - b_row
        g_col = gcol_ref[rows, :] + gbcol_ref[...]
        lf_col = _log_sigmoid(g_col[:, LANES:])
        b_col = _sel_dot(tri_le if direction == 0 else tri_ge, lf_col)
        a_col = g_col[:, :LANES] - b_col
        pmax = _scan(a_col, 0, direction == 1, jnp.maximum, -jnp.inf)
        u = jnp.maximum(m_vec, pmax)
        u_last = u[t_last:t_last + 1, :]
        m_new = b_col[t_last:t_last + 1, :] + u_last
        thr = jnp.exp(-b_col - u)
        w = jnp.exp(a_col - u_last)
        decay = jnp.exp(m_vec - u_last)

        n_new = []
        for p in range(HEADS // 2):
            lanes = slice(p * LANES, (p + 1) * LANES)
            hd = [direction * HEADS + 2 * p, direction * HEADS + 2 * p + 1]
            q2 = q_sc[rows, lanes]
            k2 = k_sc[rows, lanes]
            u_bc = [lane_bc(u, j) for j in hd]
            e_pair = jnp.concatenate(
                [jnp.where(mask, jnp.exp(jnp.broadcast_to(a_row[j:j + 1, :], (CHUNK, CHUNK)) - ub), 0.0)
                 for j, ub in zip(hd, u_bc)], axis=1)
            zeros_k = jnp.zeros_like(k2)
            kb = jnp.concatenate([jnp.where(half0, k2, zeros_k), jnp.where(half0, zeros_k, k2)], axis=0)
            pm = (_dot_nt(q2, kb) * e_pair).astype(BF16)
            v_pair = av_ref[rows, p * 2 * DV:(p + 1) * 2 * DV]
            zeros_v = jnp.zeros_like(v_pair)
            v_bd = jnp.concatenate([jnp.where(left_half, v_pair, zeros_v),
                                    jnp.where(left_half, zeros_v, v_pair)], axis=0)
            inter = jnp.where(half0, jnp.exp(lane_bc(m_vec, hd[0]) - u_bc[0]),
                              jnp.exp(lane_bc(m_vec, hd[1]) - u_bc[1]))
            qi = (q2.astype(F32) * inter).astype(BF16)
            ct = ct_sc[direction, p]
            n_row = n_rows[direction * (HEADS // 2) + p]
            n_bd = jnp.where(state_mask, jnp.broadcast_to(n_row, (2 * DV, LANES)), 0.0).astype(BF16)
            num = _dot(pm, v_bd) + _dot_nt(qi, ct.astype(BF16))
            den = _dot(pm, ones_bd) + _dot_nt(qi, n_bd)
            thr_pair = jnp.concatenate([lane_bc(thr, hd[0]), lane_bc(thr, hd[1])], axis=1)
            hsum_sc[rows, p * 2 * DV:(p + 1) * 2 * DV] += num / jnp.maximum(jnp.abs(den), thr_pair)
            w_pair = jnp.where(half0, lane_bc(w, hd[0]), lane_bc(w, hd[1]))
            wk = k2.astype(F32) * w_pair
            decay_pair = jnp.where(half0_row, lane_bc(decay, hd[0]), lane_bc(decay, hd[1]))
            upd_t = _dot_tn(v_pair, wk.astype(BF16))
            ct_sc[direction, p] = ct * decay_pair + jnp.where(state_mask, upd_t, 0.0)
            n_new.append(n_row * decay_pair + jnp.sum(wk, axis=0, keepdims=True))
        return m_new, n_new

    def chunk_body(i, carry):
        m_vec, n_rows = carry[0], carry[1:]
        m_fw, n_fw = one_dir(i, 0, m_vec, n_rows)
        m_bw, n_bw = one_dir(NCHUNK - 1 - i, 1, m_vec, n_rows)
        m_next = jnp.where(lane_row < HEADS, m_fw, m_bw)
        return (m_next,) + tuple(n_fw) + tuple(n_bw)

    init = tuple(jnp.zeros((1, LANES), F32) for _ in range(1 + 2 * (HEADS // 2)))
    lax.fori_loop(0, NCHUNK, chunk_body, init, unroll=2)

    _head_norm_gate_pass(hsum_sc, ao_ref, nw_ref, y_ref, _sigmoid)


def _mlstm(main, small, small_t, gb_row, gb_col, conv_w, conv_b, norm_w):
    def blk(j):
        return pl.BlockSpec((SEQ, V_W), lambda b: (b, j))
    full = lambda shape: pl.BlockSpec(shape, lambda b: tuple(0 for _ in shape))
    return pl.pallas_call(
        _mlstm_kernel,
        grid=(BATCH,),
        in_specs=[
            blk(0), blk(1), blk(2),
            pl.BlockSpec((SEQ, SMALL_W), lambda b: (b, 0)),
            pl.BlockSpec((NCHUNK, N_GATES, CHUNK), lambda b: (b, 0, 0)),
            full((N_GATES, CHUNK)), full((1, SMALL_W)), full((3, 2 * QK_W)), full((1, 2 * QK_W)), full((1, V_W)),
        ],
        out_specs=pl.BlockSpec((SEQ, V_W), lambda b: (b, 0)),
        out_shape=jax.ShapeDtypeStruct((TOKENS, V_W), BF16),
        scratch_shapes=[
            pltpu.VMEM((SEQ, QK_W), BF16),
            pltpu.VMEM((SEQ, QK_W), BF16),
            pltpu.VMEM((SEQ, V_W), F32),
            pltpu.VMEM((2, HEADS // 2, 2 * DV, 2 * DK), F32),
        ],
        compiler_params=pltpu.CompilerParams(
            dimension_semantics=("arbitrary",), vmem_limit_bytes=VMEM_LIMIT),
        name="mlstm",
    )(main, main, main, small, small_t, gb_row, gb_col, conv_w, conv_b, norm_w)


def _gla_kernel(bqk_ref, bv_ref, br_ref, gcol_ref, w2_ref, gb_ref, nw_ref,
                y_ref, la_sc, osum_sc, st_sc):

    def pre_body(r, tot_min):
        rows = pl.ds(pl.multiple_of(r * ROW_BLK, ROW_BLK), ROW_BLK)
        lr = gcol_ref[rows, :LANES].astype(BF16)
        for d in range(2):
            z = _dot(lr, w2_ref[d]) + gb_ref[d]
            la = _log_sigmoid(z) * (1.0 / GLA_GATE_TEMP)
            la_sc[d, rows, :] = la
            for cc in range(ROW_BLK // CHUNK):
                chunk_total = jnp.sum(la[cc * CHUNK:(cc + 1) * CHUNK, :], axis=0, keepdims=True)
                tot_min = jnp.minimum(tot_min, chunk_total)
        return tot_min
    tot_min = lax.fori_loop(0, SEQ // ROW_BLK, pre_body, jnp.zeros((1, QK_W), F32))
    all_mild = jnp.min(tot_min) >= -GLA_MILD_LOG_DECAY

    osum_sc[...] = jnp.zeros_like(osum_sc)
    st_sc[...] = jnp.zeros_like(st_sc)

    t_idx, s_idx = _tri_masks()
    eye = t_idx == s_idx
    lane2 = lax.broadcasted_iota(jnp.int32, (CHUNK, 2 * DV), 1)
    left_half = lane2 < DV
    bd_rows = lax.broadcasted_iota(jnp.int32, (2 * DV, LANES), 0) < DV
    bd_lanes = lax.broadcasted_iota(jnp.int32, (2 * DV, LANES), 1) < DK
    state_mask = bd_rows == bd_lanes

    def intra_mild(q2, k2, b_p, causal):
        qf = (q2 * jnp.exp(b_p)).astype(BF16)
        kf = (k2 * jnp.exp(-b_p)).astype(BF16)
        kb = jnp.concatenate([jnp.where(_half_mask(0), kf, jnp.zeros_like(kf)),
                              jnp.where(_half_mask(1), kf, jnp.zeros_like(kf))], axis=0)
        a_pair = _dot_nt(qf, kb)
        causal2 = jnp.concatenate([causal, causal], axis=1)
        return jnp.where(causal2, a_pair, 0.0)

    def intra_robust(q2, k2, la_p, seg_sum_p, direction, causal, strict_rev):
        k2b = k2.astype(BF16)
        acc = []
        for hh in range(2):
            qm = jnp.where(_half_mask(hh), q2, 0.0).astype(BF16)
            acc.append(jnp.where(eye, _dot_nt(qm, k2b), 0.0))
        for m in GLA_LEVELS + (1,):
            if m == 1:
                dq, dk = la_p, None
            else:
                same = _blk(t_idx, m) == _blk(s_idx, m)
                dq = seg_sum_p(same & causal)
                dk = seg_sum_p(same & strict_rev)
            qt = q2 * jnp.exp(dq)
            ktb = (k2 if dk is None else k2 * jnp.exp(dk)).astype(BF16)
            tb = _blk(t_idx, m)
            sb = _blk(s_idx, m)
            if direction == 0:
                lvl_mask = ((tb & 1) == 1) & (sb == tb - 1)
            else:
                lvl_mask = ((tb & 1) == 0) & (sb == tb + 1)
            for hh in range(2):
                qm = jnp.where(_half_mask(hh), qt, 0.0).astype(BF16)
                acc[hh] = acc[hh] + jnp.where(lvl_mask, _dot_nt(qm, ktb), 0.0)
        return jnp.concatenate(acc, axis=1)

    def one_dir(c, direction, mild):
        r0 = pl.multiple_of(c * CHUNK, CHUNK)
        rows = pl.ds(r0, CHUNK)
        causal = (t_idx >= s_idx) if direction == 0 else (t_idx <= s_idx)
        strict_rev = (s_idx > t_idx) if direction == 0 else (s_idx < t_idx)
        t_last = CHUNK - 1 if direction == 0 else 0

        la = la_sc[direction, rows, :]
        b = _scan(la, 0, direction == 1, jnp.add, 0.0)
        b_tot = b[t_last:t_last + 1, :]
        eb = jnp.exp(b)
        k_decay = jnp.exp(b_tot - b)
        state_decay = jnp.exp(b_tot)

        for p in range(HEADS // 2):
            lanes = slice(p * LANES, (p + 1) * LANES)
            q2 = bqk_ref[rows, lanes].astype(F32) * (DK ** -0.5)
            k2 = bqk_ref[rows, QK_W + p * LANES:QK_W + (p + 1) * LANES].astype(F32)
            if mild:
                a_pair = intra_mild(q2, k2, b[:, lanes], causal)
            else:
                la_p3 = _split_bf16(la[:, lanes])

                def seg_sum_p(sel):
                    selb = sel.astype(BF16)
                    return _dot(selb, la_p3[0]) + _dot(selb, la_p3[1]) + _dot(selb, la_p3[2])
                a_pair = intra_robust(q2, k2, la[:, lanes], seg_sum_p, direction, causal, strict_rev)
            v_pair = bv_ref[rows, p * 2 * DV:(p + 1) * 2 * DV]
            zeros_v = jnp.zeros_like(v_pair)
            v_bd = jnp.concatenate([jnp.where(left_half, v_pair, zeros_v),
                                    jnp.where(left_half, zeros_v, v_pair)], axis=0)
            st = st_sc[direction, p]
            q_inter = (q2 * eb[:, lanes]).astype(BF16)
            o = _dot(a_pair.astype(BF16), v_bd) + _dot_nt(q_inter, st.astype(BF16))
            osum_sc[rows, p * 2 * DV:(p + 1) * 2 * DV] += o
            k_state = (k2 * k_decay[:, lanes]).astype(BF16)
            upd_t = _dot_tn(v_pair, k_state)
            st_sc[direction, p] = st * state_decay[:, lanes] + jnp.where(state_mask, upd_t, 0.0)

    def run(mild):
        def chunk_body(i, carry):
            one_dir(i, 0, mild)
            one_dir(NCHUNK - 1 - i, 1, mild)
            return carry
        lax.fori_loop(0, NCHUNK, chunk_body, 0, unroll=2 if mild else 1)

    @pl.when(all_mild)
    def _():
        run(True)

    @pl.when(jnp.logical_not(all_mild))
    def _():
        run(False)

    _head_norm_gate_pass(osum_sc, br_ref, nw_ref, y_ref, lambda g: g * _sigmoid(g))


def _gla(main, small, w2, gb, norm_w):
    full = lambda shape: pl.BlockSpec(shape, lambda b: tuple(0 for _ in shape))
    return pl.pallas_call(
        _gla_kernel,
        grid=(BATCH,),
        in_specs=[
            pl.BlockSpec((SEQ, V_W), lambda b: (b, 3)),
            pl.BlockSpec((SEQ, V_W), lambda b: (b, 4)),
            pl.BlockSpec((SEQ, V_W), lambda b: (b, 5)),
            pl.BlockSpec((SEQ, SMALL_W), lambda b: (b, 0)),
            full((2, LANES, QK_W)), full((2, 1, QK_W)), full((1, V_W)),
        ],
        out_specs=pl.BlockSpec((SEQ, V_W), lambda b: (b, 0)),
        out_shape=jax.ShapeDtypeStruct((TOKENS, V_W), BF16),
        scratch_shapes=[
            pltpu.VMEM((2, SEQ, QK_W), F32),
            pltpu.VMEM((SEQ, V_W), F32),
            pltpu.VMEM((2, HEADS // 2, 2 * DV, 2 * DK), F32),
        ],
        compiler_params=pltpu.CompilerParams(
            dimension_semantics=("arbitrary",), vmem_limit_bytes=VMEM_LIMIT),
        name="gla",
    )(main, main, main, small, w2, gb, norm_w)


def _residual_and_next_norm(rows, x_ref, branch, g_post_ref, g_next_ref, o_ref, hn_ref):
    x_new = x_ref[rows, :] + branch * _rms_scale(branch) * g_post_ref[...]
    o_ref[rows, :] = x_new
    if hn_ref is not None:
        hn_ref[rows, :] = (x_new * _rms_scale(x_new) * g_next_ref[...]).astype(BF16)


def _row_blocks():
    return [slice(r, r + EPI_RB) for r in range(0, TM, EPI_RB)]


def _outproj_kernel(ya_ref, yb_ref, x_ref, w_ref, g_post_ref, g_next_ref, o_ref, hn_ref):
    for rows in _row_blocks():
        mix = _dot(ya_ref[rows, :], w_ref[:V_W, :]) + _dot(yb_ref[rows, :], w_ref[V_W:, :])
        _residual_and_next_norm(rows, x_ref, mix, g_post_ref, g_next_ref, o_ref, hn_ref)


def _row_tile(width):
    return pl.BlockSpec((TM, width), lambda i: (i, 0))


def _resident(shape):
    return pl.BlockSpec(shape, lambda i: (0, 0))


def _outproj(ya, yb, x2d, w, g_post, g_next):
    return pl.pallas_call(
        _outproj_kernel,
        grid=(TOKENS // TM,),
        in_specs=[_row_tile(V_W), _row_tile(V_W), _row_tile(D_MODEL), _resident((D_MODEL, D_MODEL)),
                  _resident((1, D_MODEL)), _resident((1, D_MODEL))],
        out_specs=[_row_tile(D_MODEL), _row_tile(D_MODEL)],
        out_shape=[jax.ShapeDtypeStruct((TOKENS, D_MODEL), F32),
                   jax.ShapeDtypeStruct((TOKENS, D_MODEL), BF16)],
        compiler_params=pltpu.CompilerParams(
            dimension_semantics=("arbitrary",), vmem_limit_bytes=VMEM_LIMIT),
        name="outproj",
    )(ya, yb, x2d, w, g_post, g_next)


def _gelu_tanh(x):
    c = 0.7978845608028654
    half_x = 0.5 * x
    return half_x + half_x * jnp.tanh(x * (c + (c * 0.044715) * (x * x)))


def _ffn_up_kernel(hn_ref, wg_ref, wu_ref, cw_ref, cb_ref, h_ref, g_sc, u_sc):
    g_sc[0:8, :] = jnp.zeros((8, FF_TN), F32)
    g_sc[SEQ + 8:SEQ + 16, :] = jnp.zeros((8, FF_TN), F32)
    n_blk = len(FF_BLOCKS)
    starts = [sum(FF_BLOCKS[:i]) for i in range(n_blk)]

    def project(i):
        rows = slice(starts[i], starts[i] + FF_BLOCKS[i])
        hn = hn_ref[rows, :]
        g_sc[8 + rows.start:8 + rows.stop, :] = _dot(hn, wg_ref[...])
        u_sc[rows, :] = _dot(hn, wu_ref[...])

    row = lax.broadcasted_iota(jnp.int32, (FF_ACT_RB, 1), 0)
    first_row = row == 0
    last_row = row == FF_ACT_RB - 1

    def activate(i):
        for j in range(FF_BLOCKS[i] // FF_ACT_RB):
            r0 = starts[i] + j * FF_ACT_RB
            cur = g_sc[r0 + 8:r0 + 8 + FF_ACT_RB, :]
            prev_row = g_sc[r0 + 7:r0 + 8, :]
            next_row = g_sc[r0 + 8 + FF_ACT_RB:r0 + 9 + FF_ACT_RB, :]
            prev = jnp.where(first_row, prev_row, pltpu.roll(cur, 1, 0))
            nxt = jnp.where(last_row, next_row, pltpu.roll(cur, FF_ACT_RB - 1, 0))
            gc = prev * cw_ref[0:1, :] + cur * cw_ref[1:2, :] + nxt * cw_ref[2:3, :] + cb_ref[...]
            h_ref[r0:r0 + FF_ACT_RB, :] = (_gelu_tanh(gc) * u_sc[r0:r0 + FF_ACT_RB, :]).astype(BF16)

    for i in range(n_blk + 1):
        if i < n_blk:
            project(i)
        if i >= 1:
            activate(i - 1)


def _ffn_up(hn, wg, wu, cw, cb):
    return pl.pallas_call(
        _ffn_up_kernel,
        grid=(BATCH, D_FF // FF_TN),
        in_specs=[
            pl.BlockSpec((SEQ, D_MODEL), lambda b, n: (b, 0)),
            pl.BlockSpec((D_MODEL, FF_TN), lambda b, n: (0, n)),
            pl.BlockSpec((D_MODEL, FF_TN), lambda b, n: (0, n)),
            pl.BlockSpec((3, FF_TN), lambda b, n: (0, n)),
            pl.BlockSpec((1, FF_TN), lambda b, n: (0, n)),
        ],
        out_specs=pl.BlockSpec((SEQ, FF_TN), lambda b, n: (b, n)),
        out_shape=jax.ShapeDtypeStruct((TOKENS, D_FF), BF16),
        scratch_shapes=[
            pltpu.VMEM((SEQ + 16, FF_TN), F32),
            pltpu.VMEM((SEQ, FF_TN), F32),
        ],
        compiler_params=pltpu.CompilerParams(
            dimension_semantics=("arbitrary", "arbitrary"), vmem_limit_bytes=VMEM_LIMIT),
        name="ffn_up",
    )(hn, wg, wu, cw, cb)


def _ffn_down_kernel(h_ref, x_ref, w_ref, g_post_ref, g_next_ref, o_ref, hn_ref):
    for rows in _row_blocks():
        ff = _dot(h_ref[rows, :], w_ref[...])
        _residual_and_next_norm(rows, x_ref, ff, g_post_ref, g_next_ref, o_ref, hn_ref)


def _ffn_down_last_kernel(h_ref, x_ref, w_ref, g_post_ref, o_ref):
    for rows in _row_blocks():
        ff = _dot(h_ref[rows, :], w_ref[...])
        _residual_and_next_norm(rows, x_ref, ff, g_post_ref, None, o_ref, None)


def _ffn_down(h, x2d, w, g_post, g_next):
    last = g_next is None
    in_specs = [_row_tile(D_FF), _row_tile(D_MODEL), _resident((D_FF, D_MODEL)), _resident((1, D_MODEL))]
    out_specs = [_row_tile(D_MODEL)]
    out_shape = [jax.ShapeDtypeStruct((TOKENS, D_MODEL), F32)]
    args = [h, x2d, w, g_post]
    if not last:
        in_specs.append(_resident((1, D_MODEL)))
        out_specs.append(_row_tile(D_MODEL))
        out_shape.append(jax.ShapeDtypeStruct((TOKENS, D_MODEL), BF16))
        args.append(g_next)
    outs = pl.pallas_call(
        _ffn_down_last_kernel if last else _ffn_down_kernel,
        grid=(TOKENS // TM,),
        in_specs=in_specs,
        out_specs=out_specs,
        out_shape=out_shape,
        compiler_params=pltpu.CompilerParams(
            dimension_semantics=("arbitrary",), vmem_limit_bytes=VMEM_LIMIT),
        name="ffn_down_last" if last else "ffn_down",
    )(*args)
    return (outs[0], None) if last else (outs[0], outs[1])


def _split_in_weights(w_in_l):
    offs = [0]
    for w in IN_WIDTHS:
        offs.append(offs[-1] + w)
    a_qk, a_v, a_o, gates, b_q, b_k, b_v, b_r, b_lr = (w_in_l[:, offs[i]:offs[i + 1]] for i in range(9))
    w_main = jnp.concatenate([a_qk, a_v, a_o, b_q, b_k, b_v, b_r], axis=1).astype(BF16)
    gates_i, gates_f = _gates_by_direction_head(gates, axis=1)
    pad = lambda n: jnp.zeros((D_MODEL, n), w_in_l.dtype)
    w_small = jnp.concatenate([gates_i, pad(LR_OFF - 2 * HEADS), b_lr, pad(LANES - LR_OFF - 2 * GLA_RANK),
                               gates_f, pad(SMALL_W - LANES - 2 * HEADS)], axis=1).astype(BF16)
    w_gates_t = jnp.concatenate([gates_i, gates_f], axis=1).T.astype(BF16)
    return w_main, w_small, w_gates_t


def _gates_by_direction_head(g, axis):
    i_fw, f_fw, i_bw, f_bw = jnp.split(g, 4, axis=axis)
    return jnp.concatenate([i_fw, i_bw], axis=axis), jnp.concatenate([f_fw, f_bw], axis=axis)


def kernel(x, norm_mix_pre, norm_mix_post, norm_ffn_pre, norm_ffn_post, w_in, mlstm_gate_b, mlstm_conv_w,
           mlstm_conv_b, mlstm_norm, gla_w2, gla_b, gla_norm, w_out, ffn_w_gate, ffn_w_up, ffn_conv_w,
           ffn_conv_b, ffn_w_down):
    def row(v):
        return v.reshape(1, -1).astype(F32)

    x2d = x.reshape(TOKENS, D_MODEL).astype(F32)
    hn = _rmsnorm(x2d, row(norm_mix_pre[0]))
    for l in range(DEPTH):
        w_main, w_small, w_small_t = _split_in_weights(w_in[l])
        main, small, small_t = _inproj(hn, w_main, w_small, w_small_t)

        bias_i, bias_f = _gates_by_direction_head(mlstm_gate_b[l].astype(F32), axis=0)
        gb_row = jnp.broadcast_to(jnp.concatenate([bias_i, bias_f])[:, None], (N_GATES, CHUNK))
        gb_col = jnp.concatenate([bias_i, jnp.zeros((LANES - 2 * HEADS,), F32),
                                  bias_f, jnp.zeros((SMALL_W - LANES - 2 * HEADS,), F32)]).reshape(1, SMALL_W)
        y_a = _mlstm(main, small, small_t, gb_row, gb_col,
                     mlstm_conv_w[l].astype(F32), row(mlstm_conv_b[l]), row(mlstm_norm[l]))

        w2 = jnp.zeros((2, LANES, QK_W), F32)
        w2 = w2.at[0, LR_OFF:LR_OFF + GLA_RANK].set(gla_w2[l, 0])
        w2 = w2.at[1, LR_OFF + GLA_RANK:LR_OFF + 2 * GLA_RANK].set(gla_w2[l, 1])
        y_b = _gla(main, small, w2.astype(BF16), gla_b[l].reshape(2, 1, QK_W).astype(F32), row(gla_norm[l]))

        x2d, hn = _outproj(y_a, y_b, x2d, w_out[l].astype(BF16), row(norm_mix_post[l]), row(norm_ffn_pre[l]))

        h = _ffn_up(hn, ffn_w_gate[l].astype(BF16), ffn_w_up[l].astype(BF16),
                    ffn_conv_w[l].astype(F32), row(ffn_conv_b[l]))
        g_next = row(norm_mix_pre[l + 1]) if l + 1 < DEPTH else None
        x2d, hn = _ffn_down(h, x2d, ffn_w_down[l].astype(BF16), row(norm_ffn_post[l]), g_next)
    return x2d.reshape(BATCH, SEQ, D_MODEL)
```

```python
import functools

import jax
import jax.numpy as jnp
from jax import lax
from jax.experimental import pallas as pl
from jax.experimental.pallas import tpu as pltpu

F32 = jnp.float32
BF16 = jnp.bfloat16

D_MODEL = 1024
BATCH = 8
SEQ = 2048
DEPTH = 4
TOKENS = BATCH * SEQ
HEADS = 4
DK = 64
DV = 128
QK_W = HEADS * DK
V_W = HEADS * DV
GLA_RANK = 16
GLA_GATE_TEMP = 16.0
D_FF = 2816
EPS = 1e-6
IN_WIDTHS = (2 * QK_W, V_W, V_W, 4 * HEADS, QK_W, QK_W, V_W, V_W, 2 * GLA_RANK)
N_MAIN = 2 * QK_W + V_W + V_W + QK_W + QK_W + V_W + V_W
N_GATES = 4 * HEADS
SMALL_W = 256
LR_OFF = 16

LANES = 128
CHUNK = 128
NCHUNK = SEQ // CHUNK
ROW_BLK = 256
TM = 1024
EPI_RB = 256
FF_TN = 256
FF_BLOCKS = (512, 512, 512, 512)
assert sum(FF_BLOCKS) == SEQ
FF_ACT_RB = 128
GLA_LEVELS = (64, 32, 16, 8, 4, 2)
GLA_MILD_LOG_DECAY = 60.0
VMEM_LIMIT = 56 * 1024 * 1024


def _dot(a, b):
    return jnp.dot(a, b, preferred_element_type=F32)


def _dot_nt(a, b):
    return lax.dot_general(a, b, (((1,), (1,)), ((), ())), preferred_element_type=F32)


def _dot_tn(a, b):
    return lax.dot_general(a, b, (((0,), (0,)), ((), ())), preferred_element_type=F32)


def _scan(x, axis, reverse, op, identity):
    n = x.shape[axis]
    idx = lax.broadcasted_iota(jnp.int32, x.shape, axis)
    k = 1
    while k < n:
        if reverse:
            shifted = jnp.where(idx < n - k, pltpu.roll(x, n - k, axis), identity)
        else:
            shifted = jnp.where(idx >= k, pltpu.roll(x, k, axis), identity)
        x = op(x, shifted)
        k *= 2
    return x


def _split_bf16(x):
    hi = x.astype(BF16)
    r1 = x - hi.astype(F32)
    mid = r1.astype(BF16)
    lo = (r1 - mid.astype(F32)).astype(BF16)
    return hi, mid, lo


def _sel_dot(sel, x):
    hi, mid, lo = _split_bf16(x)
    return _dot(sel, hi) + _dot(sel, mid) + _dot(sel, lo)


def _dot_sel(x, sel):
    hi, mid, lo = _split_bf16(x)
    return _dot(hi, sel) + _dot(mid, sel) + _dot(lo, sel)


def _log_sigmoid(x):
    return jnp.minimum(x, 0.0) - jnp.log(1.0 + jnp.exp(-jnp.abs(x)))


def _sigmoid(x):
    return 1.0 / (1.0 + jnp.exp(-x))


def _rms_scale(x):
    return lax.rsqrt(jnp.mean(x * x, axis=-1, keepdims=True) + EPS)


def _rmsnorm_kernel(x_ref, g_ref, o_ref):
    x = x_ref[...]
    o_ref[...] = (x * _rms_scale(x) * g_ref[...]).astype(BF16)


def _rmsnorm(x2d, g):
    return pl.pallas_call(
        _rmsnorm_kernel,
        grid=(TOKENS // TM,),
        in_specs=[
            pl.BlockSpec((TM, D_MODEL), lambda i: (i, 0)),
            pl.BlockSpec((1, D_MODEL), lambda i: (0, 0)),
        ],
        out_specs=pl.BlockSpec((TM, D_MODEL), lambda i: (i, 0)),
        out_shape=jax.ShapeDtypeStruct((TOKENS, D_MODEL), BF16),
        compiler_params=pltpu.CompilerParams(
            dimension_semantics=("arbitrary",), vmem_limit_bytes=VMEM_LIMIT),
        name="rmsnorm",
    )(x2d, g)


def _inproj_kernel(hn_ref, wm_ref, ws_ref, wst_ref, main_ref, small_ref, smallt_ref):
    hn = hn_ref[...]
    for j in range(N_MAIN // 512):
        cols = slice(j * 512, (j + 1) * 512)
        main_ref[:, cols] = _dot(hn, wm_ref[:, cols]).astype(BF16)
    small_ref[...] = _dot(hn, ws_ref[...])
    st = _dot_nt(wst_ref[...], hn)
    for j in range(TM // CHUNK):
        smallt_ref[j] = st[:, j * CHUNK:(j + 1) * CHUNK]


def _inproj(hn, w_main, w_small, w_small_t):
    return pl.pallas_call(
        _inproj_kernel,
        grid=(TOKENS // TM,),
        in_specs=[
            pl.BlockSpec((TM, D_MODEL), lambda i: (i, 0)),
            pl.BlockSpec((D_MODEL, N_MAIN), lambda i: (0, 0)),
            pl.BlockSpec((D_MODEL, SMALL_W), lambda i: (0, 0)),
            pl.BlockSpec((N_GATES, D_MODEL), lambda i: (0, 0)),
        ],
        out_specs=[
            pl.BlockSpec((TM, N_MAIN), lambda i: (i, 0)),
            pl.BlockSpec((TM, SMALL_W), lambda i: (i, 0)),
            pl.BlockSpec((TM // CHUNK, N_GATES, CHUNK), lambda i: (i, 0, 0)),
        ],
        out_shape=[
            jax.ShapeDtypeStruct((TOKENS, N_MAIN), BF16),
            jax.ShapeDtypeStruct((TOKENS, SMALL_W), F32),
            jax.ShapeDtypeStruct((TOKENS // CHUNK, N_GATES, CHUNK), F32),
        ],
        compiler_params=pltpu.CompilerParams(
            dimension_semantics=("arbitrary",), vmem_limit_bytes=VMEM_LIMIT),
        name="inproj",
    )(hn, w_main, w_small, w_small_t)


def _tri_masks():
    t = lax.broadcasted_iota(jnp.int32, (CHUNK, CHUNK), 0)
    s = lax.broadcasted_iota(jnp.int32, (CHUNK, CHUNK), 1)
    return t, s


def _blk(idx, m):
    return jnp.right_shift(idx, m.bit_length() - 1)


def _half_mask(hh):
    lane = lax.broadcasted_iota(jnp.int32, (CHUNK, LANES), 1)
    return (lane < DK) if hh == 0 else (lane >= DK)


def _head_norm_gate_pass(sum_sc, gate_ref, nw_ref, y_ref, gate_fn):
    def body(r, carry):
        rows = pl.ds(pl.multiple_of(r * ROW_BLK, ROW_BLK), ROW_BLK)
        for h in range(HEADS):
            cols = slice(h * DV, (h + 1) * DV)
            hs = sum_sc[rows, cols]
            yn = hs * _rms_scale(hs) * nw_ref[:, cols]
            gate = gate_ref[rows, cols].astype(F32)
            y_ref[rows, cols] = (gate_fn(gate) * yn).astype(BF16)
        return carry
    lax.fori_loop(0, SEQ // ROW_BLK, body, 0)


def _mlstm_parts(aqk_ref, av_ref, ao_ref, gcol_ref, grow_ref, gbrow_ref, gbcol_ref, cw_ref, cb_ref, nw_ref,
                 y_ref, q_sc, k_sc, hsum_sc, ct_sc):

    def conv_body(r, carry):
        r0 = pl.multiple_of(r * ROW_BLK, ROW_BLK)
        cur = aqk_ref[pl.ds(r0, ROW_BLK), :].astype(F32)
        p0 = pl.multiple_of(jnp.maximum(r0 - 16, 0), 16)
        n0 = pl.multiple_of(jnp.minimum(r0 + ROW_BLK, SEQ - 16), 16)
        prev_row = aqk_ref[pl.ds(p0, 16), :].astype(F32)[15:16]
        next_row = aqk_ref[pl.ds(n0, 16), :].astype(F32)[0:1]
        prev_row = jnp.where(r == 0, 0.0, prev_row)
        next_row = jnp.where(r == SEQ // ROW_BLK - 1, 0.0, next_row)
        row = lax.broadcasted_iota(jnp.int32, (ROW_BLK, 1), 0)
        prev = jnp.where(row == 0, prev_row, pltpu.roll(cur, 1, 0))
        nxt = jnp.where(row == ROW_BLK - 1, next_row, pltpu.roll(cur, ROW_BLK - 1, 0))
        c = prev * cw_ref[0:1, :] + cur * cw_ref[1:2, :] + nxt * cw_ref[2:3, :] + cb_ref[...]
        act = c * _sigmoid(c)
        q_sc[pl.ds(r0, ROW_BLK), :] = act[:, :QK_W].astype(BF16)
        k_sc[pl.ds(r0, ROW_BLK), :] = (act[:, QK_W:] * (DK ** -0.5)).astype(BF16)
        return carry
    lax.fori_loop(0, SEQ // ROW_BLK, conv_body, 0)

    hsum_sc[...] = jnp.zeros_like(hsum_sc)
    ct_sc[...] = jnp.zeros_like(ct_sc)

    t_idx, s_idx = _tri_masks()
    le = t_idx >= s_idx
    ge = t_idx <= s_idx
    tri_le = le.astype(BF16)
    tri_ge = ge.astype(BF16)
    lane_row = lax.broadcasted_iota(jnp.int32, (1, LANES), 1)
    half0 = _half_mask(0)
    half0_row = lane_row < DK
    lane2 = lax.broadcasted_iota(jnp.int32, (CHUNK, 2 * DV), 1)
    left_half = lane2 < DV
    bd_r = lax.broadcasted_iota(jnp.int32, (2 * DV, 2 * DV), 0) < DV
    bd_l = lax.broadcasted_iota(jnp.int32, (2 * DV, 2 * DV), 1) < DV
    ones_bd = (bd_r == bd_l).astype(BF16)
    st_r = lax.broadcasted_iota(jnp.int32, (2 * DV, LANES), 0) < DV
    st_l = lax.broadcasted_iota(jnp.int32, (2 * DV, LANES), 1) < DK
    state_mask = st_r == st_l

    def lane_bc(x, j):
        return jnp.broadcast_to(x[:, j:j + 1], (x.shape[0], LANES))

    def one_dir(c, direction, m_vec, n_rows):
        r0 = pl.multiple_of(c * CHUNK, CHUNK)
        rows = pl.ds(r0, CHUNK)
        mask = le if direction == 0 else ge
        t_last = CHUNK - 1 if direction == 0 else 0

        g_row = grow_ref[c] + gbrow_ref[...]
        lf_row = _log_sigmoid(g_row[2 * HEADS:, :])
        b_row = _dot_sel(lf_row, tri_ge if direction == 0 else tri_le)
        a_row = g_row[:2 * HEADS, :] - b_row
        g_col = gcol_ref[rows, :] + gbcol_ref[...]
        lf_col = _log_sigmoid(g_col[:, LANES:])
        b_col = _sel_dot(tri_le if direction == 0 else tri_ge, lf_col)
        a_col = g_col[:, :LANES] - b_col
        pmax = _scan(a_col, 0, direction == 1, jnp.maximum, -jnp.inf)
        u = jnp.maximum(m_vec, pmax)
        u_last = u[t_last:t_last + 1, :]
        m_new = b_col[t_last:t_last + 1, :] + u_last
        thr = jnp.exp(-b_col - u)
        w = jnp.exp(a_col - u_last)
        decay = jnp.exp(m_vec - u_last)

        n_new = []
        for p in range(HEADS // 2):
            lanes = slice(p * LANES, (p + 1) * LANES)
            hd = [direction * HEADS + 2 * p, direction * HEADS + 2 * p + 1]
            q2 = q_sc[rows, lanes]
            k2 = k_sc[rows, lanes]
            u_bc = [lane_bc(u, j) for j in hd]
            e_pair = jnp.concatenate(
                [jnp.where(mask, jnp.exp(jnp.broadcast_to(a_row[j:j + 1, :], (CHUNK, CHUNK)) - ub), 0.0)
                 for j, ub in zip(hd, u_bc)], axis=1)
            zeros_k = jnp.zeros_like(k2)
            kb = jnp.concatenate([jnp.where(half0, k2, zeros_k), jnp.where(half0, zeros_k, k2)], axis=0)
            pm = (_dot_nt(q2, kb) * e_pair).astype(BF16)
            v_pair = av_ref[rows, p * 2 * DV:(p + 1) * 2 * DV]
            zeros_v = jnp.zeros_like(v_pair)
            v_bd = jnp.concatenate([jnp.where(left_half, v_pair, zeros_v),
                                    jnp.where(left_half, zeros_v, v_pair)], axis=0)
            inter = jnp.where(half0, jnp.exp(lane_bc(m_vec, hd[0]) - u_bc[0]),
                              jnp.exp(lane_bc(m_vec, hd[1]) - u_bc[1]))
            qi = (q2.astype(F32) * inter).astype(BF16)
            ct = ct_sc[direction, p]
            n_row = n_rows[direction * (HEADS // 2) + p]
            n_bd = jnp.where(state_mask, jnp.broadcast_to(n_row, (2 * DV, LANES)), 0.0).astype(BF16)
            num = _dot(pm, v_bd) + _dot_nt(qi, ct.astype(BF16))
            den = _dot(pm, ones_bd) + _dot_nt(qi, n_bd)
            thr_pair = jnp.concatenate([lane_bc(thr, hd[0]), lane_bc(thr, hd[1])], axis=1)
            hsum_sc[rows, p * 2 * DV:(p + 1) * 2 * DV] += num / jnp.maximum(jnp.abs(den), thr_pair)
            w_pair = jnp.where(half0, lane_bc(w, hd[0]), lane_bc(w, hd[1]))
            wk = k2.astype(F32) * w_pair
            decay_pair = jnp.where(half0_row, lane_bc(decay, hd[0]), lane_bc(decay, hd[1]))
            upd_t = _dot_tn(v_pair, wk.astype(BF16))
            ct_sc[direction, p] = ct * decay_pair + jnp.where(state_mask, upd_t, 0.0)
            n_new.append(n_row * decay_pair + jnp.sum(wk, axis=0, keepdims=True))
        return m_new, n_new

    def chunk_body(i, carry):
        m_vec, n_rows = carry[0], carry[1:]
        m_fw, n_fw = one_dir(i, 0, m_vec, n_rows)
        m_bw, n_bw = one_dir(NCHUNK - 1 - i, 1, m_vec, n_rows)
        m_next = jnp.where(lane_row < HEADS, m_fw, m_bw)
        return (m_next,) + tuple(n_fw) + tuple(n_bw)

    init = tuple(jnp.zeros((1, LANES), F32) for _ in range(1 + 2 * (HEADS // 2)))

    def final():
        _head_norm_gate_pass(hsum_sc, ao_ref, nw_ref, y_ref, _sigmoid)

    return init, chunk_body, final


def _gla_parts(bqk_ref, bv_ref, br_ref, gcol_ref, w2_ref, gb_ref, nw_ref,
               y_ref, la_sc, osum_sc, st_sc):

    def pre_body(r, tot_min):
        rows = pl.ds(pl.multiple_of(r * ROW_BLK, ROW_BLK), ROW_BLK)
        lr = gcol_ref[rows, :LANES].astype(BF16)
        for d in range(2):
            z = _dot(lr, w2_ref[d]) + gb_ref[d]
            la = _log_sigmoid(z) * (1.0 / GLA_GATE_TEMP)
            la_sc[d, rows, :] = la
            for cc in range(ROW_BLK // CHUNK):
                chunk_total = jnp.sum(la[cc * CHUNK:(cc + 1) * CHUNK, :], axis=0, keepdims=True)
                tot_min = jnp.minimum(tot_min, chunk_total)
        return tot_min
    tot_min = lax.fori_loop(0, SEQ // ROW_BLK, pre_body, jnp.zeros((1, QK_W), F32))
    all_mild = jnp.min(tot_min) >= -GLA_MILD_LOG_DECAY

    osum_sc[...] = jnp.zeros_like(osum_sc)
    st_sc[...] = jnp.zeros_like(st_sc)

    t_idx, s_idx = _tri_masks()
    eye = t_idx == s_idx
    lane2 = lax.broadcasted_iota(jnp.int32, (CHUNK, 2 * DV), 1)
    left_half = lane2 < DV
    bd_rows = lax.broadcasted_iota(jnp.int32, (2 * DV, LANES), 0) < DV
    bd_lanes = lax.broadcasted_iota(jnp.int32, (2 * DV, LANES), 1) < DK
    state_mask = bd_rows == bd_lanes

    def intra_mild(q2, k2, b_p, causal):
        qf = (q2 * jnp.exp(b_p)).astype(BF16)
        kf = (k2 * jnp.exp(-b_p)).astype(BF16)
        kb = jnp.concatenate([jnp.where(_half_mask(0), kf, jnp.zeros_like(kf)),
                              jnp.where(_half_mask(1), kf, jnp.zeros_like(kf))], axis=0)
        a_pair = _dot_nt(qf, kb)
        causal2 = jnp.concatenate([causal, causal], axis=1)
        return jnp.where(causal2, a_pair, 0.0)

    def intra_robust(q2, k2, la_p, seg_sum_p, direction, causal, strict_rev):
        k2b = k2.astype(BF16)
        acc = []
        for hh in range(2):
            qm = jnp.where(_half_mask(hh), q2, 0.0).astype(BF16)
            acc.append(jnp.where(eye, _dot_nt(qm, k2b), 0.0))
        for m in GLA_LEVELS + (1,):
            if m == 1:
                dq, dk = la_p, None
            else:
                same = _blk(t_idx, m) == _blk(s_idx, m)
                dq = seg_sum_p(same & causal)
                dk = seg_sum_p(same & strict_rev)
            qt = q2 * jnp.exp(dq)
            ktb = (k2 if dk is None else k2 * jnp.exp(dk)).astype(BF16)
            tb = _blk(t_idx, m)
            sb = _blk(s_idx, m)
            if direction == 0:
                lvl_mask = ((tb & 1) == 1) & (sb == tb - 1)
            else:
                lvl_mask = ((tb & 1) == 0) & (sb == tb + 1)
            for hh in range(2):
                qm = jnp.where(_half_mask(hh), qt, 0.0).astype(BF16)
                acc[hh] = acc[hh] + jnp.where(lvl_mask, _dot_nt(qm, ktb), 0.0)
        return jnp.concatenate(acc, axis=1)

    def one_dir(c, direction, mild):
        r0 = pl.multiple_of(c * CHUNK, CHUNK)
        rows = pl.ds(r0, CHUNK)
        causal = (t_idx >= s_idx) if direction == 0 else (t_idx <= s_idx)
        strict_rev = (s_idx > t_idx) if direction == 0 else (s_idx < t_idx)
        t_last = CHUNK - 1 if direction == 0 else 0

        la = la_sc[direction, rows, :]
        b = _scan(la, 0, direction == 1, jnp.add, 0.0)
        b_tot = b[t_last:t_last + 1, :]
        eb = jnp.exp(b)
        k_decay = jnp.exp(b_tot - b)
        state_decay = jnp.exp(b_tot)

        for p in range(HEADS // 2):
            lanes = slice(p * LANES, (p + 1) * LANES)
            q2 = bqk_ref[rows, lanes].astype(F32) * (DK ** -0.5)
            k2 = bqk_ref[rows, QK_W + p * LANES:QK_W + (p + 1) * LANES].astype(F32)
            if mild:
                a_pair = intra_mild(q2, k2, b[:, lanes], causal)
            else:
                la_p3 = _split_bf16(la[:, lanes])

                def seg_sum_p(sel):
                    selb = sel.astype(BF16)
                    return _dot(selb, la_p3[0]) + _dot(selb, la_p3[1]) + _dot(selb, la_p3[2])
                a_pair = intra_robust(q2, k2, la[:, lanes], seg_sum_p, direction, causal, strict_rev)
            v_pair = bv_ref[rows, p * 2 * DV:(p + 1) * 2 * DV]
            zeros_v = jnp.zeros_like(v_pair)
            v_bd = jnp.concatenate([jnp.where(left_half, v_pair, zeros_v),
                                    jnp.where(left_half, zeros_v, v_pair)], axis=0)
            st = st_sc[direction, p]
            q_inter = (q2 * eb[:, lanes]).astype(BF16)
            o = _dot(a_pair.astype(BF16), v_bd) + _dot_nt(q_inter, st.astype(BF16))
            osum_sc[rows, p * 2 * DV:(p + 1) * 2 * DV] += o
            k_state = (k2 * k_decay[:, lanes]).astype(BF16)
            upd_t = _dot_tn(v_pair, k_state)
            st_sc[direction, p] = st * state_decay[:, lanes] + jnp.where(state_mask, upd_t, 0.0)

    def chunk_step(i, mild):
        one_dir(i, 0, mild)
        one_dir(NCHUNK - 1 - i, 1, mild)

    def final():
        _head_norm_gate_pass(osum_sc, br_ref, nw_ref, y_ref, lambda g: g * _sigmoid(g))

    return all_mild, chunk_step, final


def _mixers_kernel(aqk_ref, av_ref, ao_ref, bqk_ref, bv_ref, br_ref, gcol_ref, grow_ref, gbrow_ref, gbcol_ref,
                   cw_ref, cb_ref, nwa_ref, w2_ref, gb_ref, nwb_ref, ya_ref, yb_ref,
                   q_sc, k_sc, hsum_sc, ct_sc, la_sc, osum_sc, st_sc):
    m_init, m_step, m_final = _mlstm_parts(aqk_ref, av_ref, ao_ref, gcol_ref, grow_ref, gbrow_ref, gbcol_ref,
                                           cw_ref, cb_ref, nwa_ref, ya_ref, q_sc, k_sc, hsum_sc, ct_sc)
    all_mild, g_step, g_final = _gla_parts(bqk_ref, bv_ref, br_ref, gcol_ref, w2_ref, gb_ref, nwb_ref,
                                           yb_ref, la_sc, osum_sc, st_sc)

    def run(mild):
        def chunk_body(i, carry):
            g_step(i, mild)
            return m_step(i, carry)
        lax.fori_loop(0, NCHUNK, chunk_body, m_init, unroll=2 if mild else 1)

    @pl.when(all_mild)
    def _():
        run(True)

    @pl.when(jnp.logical_not(all_mild))
    def _():
        run(False)

    m_final()
    g_final()


def _mixers(main, small, small_t, gb_row, gb_col, conv_w, conv_b, norm_a, w2, gb, norm_b):
    def blk(j):
        return pl.BlockSpec((SEQ, V_W), lambda b: (b, j))
    full = lambda shape: pl.BlockSpec(shape, lambda b: tuple(0 for _ in shape))
    return pl.pallas_call(
        _mixers_kernel,
        grid=(BATCH,),
        in_specs=[
            blk(0), blk(1), blk(2), blk(3), blk(4), blk(5),
            pl.BlockSpec((SEQ, SMALL_W), lambda b: (b, 0)),
            pl.BlockSpec((NCHUNK, N_GATES, CHUNK), lambda b: (b, 0, 0)),
            full((N_GATES, CHUNK)), full((1, SMALL_W)), full((3, 2 * QK_W)), full((1, 2 * QK_W)), full((1, V_W)),
            full((2, LANES, QK_W)), full((2, 1, QK_W)), full((1, V_W)),
        ],
        out_specs=[pl.BlockSpec((SEQ, V_W), lambda b: (b, 0)), pl.BlockSpec((SEQ, V_W), lambda b: (b, 0))],
        out_shape=[jax.ShapeDtypeStruct((TOKENS, V_W), BF16), jax.ShapeDtypeStruct((TOKENS, V_W), BF16)],
        scratch_shapes=[
            pltpu.VMEM((SEQ, QK_W), BF16),
            pltpu.VMEM((SEQ, QK_W), BF16),
            pltpu.VMEM((SEQ, V_W), F32),
            pltpu.VMEM((2, HEADS // 2, 2 * DV, 2 * DK), F32),
            pltpu.VMEM((2, SEQ, QK_W), F32),
            pltpu.VMEM((SEQ, V_W), F32),
            pltpu.VMEM((2, HEADS // 2, 2 * DV, 2 * DK), F32),
        ],
        compiler_params=pltpu.CompilerParams(
            dimension_semantics=("arbitrary",), vmem_limit_bytes=VMEM_LIMIT),
        name="mixers",
    )(main, main, main, main, main, main, small, small_t, gb_row, gb_col, conv_w, conv_b, norm_a, w2, gb, norm_b)


def _residual_and_next_norm(rows, x_ref, branch, g_post_ref, g_next_ref, o_ref, hn_ref):
    x_new = x_ref[rows, :] + branch * _rms_scale(branch) * g_post_ref[...]
    o_ref[rows, :] = x_new
    if hn_ref is not None:
        hn_ref[rows, :] = (x_new * _rms_scale(x_new) * g_next_ref[...]).astype(BF16)


def _row_blocks():
    return [slice(r, r + EPI_RB) for r in range(0, TM, EPI_RB)]


def _outproj_kernel(ya_ref, yb_ref, x_ref, w_ref, g_post_ref, g_next_ref, o_ref, hn_ref):
    for rows in _row_blocks():
        mix = _dot(ya_ref[rows, :], w_ref[:V_W, :]) + _dot(yb_ref[rows, :], w_ref[V_W:, :])
        _residual_and_next_norm(rows, x_ref, mix, g_post_ref, g_next_ref, o_ref, hn_ref)


def _row_tile(width):
    return pl.BlockSpec((TM, width), lambda i: (i, 0))


def _resident(shape):
    return pl.BlockSpec(shape, lambda i: (0, 0))


def _outproj(ya, yb, x2d, w, g_post, g_next):
    return pl.pallas_call(
        _outproj_kernel,
        grid=(TOKENS // TM,),
        in_specs=[_row_tile(V_W), _row_tile(V_W), _row_tile(D_MODEL), _resident((D_MODEL, D_MODEL)),
                  _resident((1, D_MODEL)), _resident((1, D_MODEL))],
        out_specs=[_row_tile(D_MODEL), _row_tile(D_MODEL)],
        out_shape=[jax.ShapeDtypeStruct((TOKENS, D_MODEL), F32),
                   jax.ShapeDtypeStruct((TOKENS, D_MODEL), BF16)],
        compiler_params=pltpu.CompilerParams(
            dimension_semantics=("arbitrary",), vmem_limit_bytes=VMEM_LIMIT),
        name="outproj",
    )(ya, yb, x2d, w, g_post, g_next)


def _gelu_tanh(x):
    k1 = -2.0 * 0.7978845608028654 * 1.4426950408889634
    return x / (1.0 + jnp.exp2(x * (k1 + (k1 * 0.044715) * (x * x))))


def _ffn_up_kernel(hn_ref, wg_ref, wu_ref, cw_ref, cb_ref, h_ref, g_sc, u_sc):
    g_sc[0:8, :] = jnp.zeros((8, FF_TN), F32)
    g_sc[SEQ + 8:SEQ + 16, :] = jnp.zeros((8, FF_TN), F32)
    n_blk = len(FF_BLOCKS)
    starts = [sum(FF_BLOCKS[:i]) for i in range(n_blk)]

    def project(i):
        rows = slice(starts[i], starts[i] + FF_BLOCKS[i])
        hn = hn_ref[rows, :]
        g_sc[8 + rows.start:8 + rows.stop, :] = _dot(hn, wg_ref[...])
        u_sc[rows, :] = _dot(hn, wu_ref[...])

    row = lax.broadcasted_iota(jnp.int32, (FF_ACT_RB, 1), 0)
    first_row = row == 0
    last_row = row == FF_ACT_RB - 1

    def activate(i):
        for j in range(FF_BLOCKS[i] // FF_ACT_RB):
            r0 = starts[i] + j * FF_ACT_RB
            cur = g_sc[r0 + 8:r0 + 8 + FF_ACT_RB, :]
            prev_row = g_sc[r0 + 7:r0 + 8, :]
            next_row = g_sc[r0 + 8 + FF_ACT_RB:r0 + 9 + FF_ACT_RB, :]
            prev = jnp.where(first_row, prev_row, pltpu.roll(cur, 1, 0))
            nxt = jnp.where(last_row, next_row, pltpu.roll(cur, FF_ACT_RB - 1, 0))
            gc = prev * cw_ref[0:1, :] + cur * cw_ref[1:2, :] + nxt * cw_ref[2:3, :] + cb_ref[...]
            h_ref[r0:r0 + FF_ACT_RB, :] = (_gelu_tanh(gc) * u_sc[r0:r0 + FF_ACT_RB, :]).astype(BF16)

    for i in range(n_blk + 1):
        if i < n_blk:
            project(i)
        if i >= 1:
            activate(i - 1)


def _ffn_up(hn, wg, wu, cw, cb):
    return pl.pallas_call(
        _ffn_up_kernel,
        grid=(BATCH, D_FF // FF_TN),
        in_specs=[
            pl.BlockSpec((SEQ, D_MODEL), lambda b, n: (b, 0)),
            pl.BlockSpec((D_MODEL, FF_TN), lambda b, n: (0, n)),
            pl.BlockSpec((D_MODEL, FF_TN), lambda b, n: (0, n)),
            pl.BlockSpec((3, FF_TN), lambda b, n: (0, n)),
            pl.BlockSpec((1, FF_TN), lambda b, n: (0, n)),
        ],
        out_specs=pl.BlockSpec((SEQ, FF_TN), lambda b, n: (b, n)),
        out_shape=jax.ShapeDtypeStruct((TOKENS, D_FF), BF16),
        scratch_shapes=[
            pltpu.VMEM((SEQ + 16, FF_TN), F32),
            pltpu.VMEM((SEQ, FF_TN), F32),
        ],
        compiler_params=pltpu.CompilerParams(
            dimension_semantics=("arbitrary", "arbitrary"), vmem_limit_bytes=VMEM_LIMIT),
        name="ffn_up",
    )(hn, wg, wu, cw, cb)


def _ffn_down_kernel(h_ref, x_ref, w_ref, g_post_ref, g_next_ref, o_ref, hn_ref):
    for rows in _row_blocks():
        ff = _dot(h_ref[rows, :], w_ref[...])
        _residual_and_next_norm(rows, x_ref, ff, g_post_ref, g_next_ref, o_ref, hn_ref)


def _ffn_down_last_kernel(h_ref, x_ref, w_ref, g_post_ref, o_ref):
    for rows in _row_blocks():
        ff = _dot(h_ref[rows, :], w_ref[...])
        _residual_and_next_norm(rows, x_ref, ff, g_post_ref, None, o_ref, None)


def _ffn_down(h, x2d, w, g_post, g_next):
    last = g_next is None
    in_specs = [_row_tile(D_FF), _row_tile(D_MODEL), _resident((D_FF, D_MODEL)), _resident((1, D_MODEL))]
    out_specs = [_row_tile(D_MODEL)]
    out_shape = [jax.ShapeDtypeStruct((TOKENS, D_MODEL), F32)]
    args = [h, x2d, w, g_post]
    if not last:
        in_specs.append(_resident((1, D_MODEL)))
        out_specs.append(_row_tile(D_MODEL))
        out_shape.append(jax.ShapeDtypeStruct((TOKENS, D_MODEL), BF16))
        args.append(g_next)
    outs = pl.pallas_call(
        _ffn_down_last_kernel if last else _ffn_down_kernel,
        grid=(TOKENS // TM,),
        in_specs=in_specs,
        out_specs=out_specs,
        out_shape=out_shape,
        compiler_params=pltpu.CompilerParams(
            dimension_semantics=("arbitrary",), vmem_limit_bytes=VMEM_LIMIT),
        name="ffn_down_last" if last else "ffn_down",
    )(*args)
    return (outs[0], None) if last else (outs[0], outs[1])


def _split_in_weights(w_in_l):
    offs = [0]
    for w in IN_WIDTHS:
        offs.append(offs[-1] + w)
    a_qk, a_v, a_o, gates, b_q, b_k, b_v, b_r, b_lr = (w_in_l[:, offs[i]:offs[i + 1]] for i in range(9))
    w_main = jnp.concatenate([a_qk, a_v, a_o, b_q, b_k, b_v, b_r], axis=1).astype(BF16)
    gates_i, gates_f = _gates_by_direction_head(gates, axis=1)
    pad = lambda n: jnp.zeros((D_MODEL, n), w_in_l.dtype)
    w_small = jnp.concatenate([gates_i, pad(LR_OFF - 2 * HEADS), b_lr, pad(LANES - LR_OFF - 2 * GLA_RANK),
                               gates_f, pad(SMALL_W - LANES - 2 * HEADS)], axis=1).astype(BF16)
    w_gates_t = jnp.concatenate([gates_i, gates_f], axis=1).T.astype(BF16)
    return w_main, w_small, w_gates_t


def _gates_by_direction_head(g, axis):
    i_fw, f_fw, i_bw, f_bw = jnp.split(g, 4, axis=axis)
    return jnp.concatenate([i_fw, i_bw], axis=axis), jnp.concatenate([f_fw, f_bw], axis=axis)


def kernel(x, norm_mix_pre, norm_mix_post, norm_ffn_pre, norm_ffn_post, w_in, mlstm_gate_b, mlstm_conv_w,
           mlstm_conv_b, mlstm_norm, gla_w2, gla_b, gla_norm, w_out, ffn_w_gate, ffn_w_up, ffn_conv_w,
           ffn_conv_b, ffn_w_down):
    def row(v):
        return v.reshape(1, -1).astype(F32)

    x2d = x.reshape(TOKENS, D_MODEL).astype(F32)
    hn = _rmsnorm(x2d, row(norm_mix_pre[0]))
    for l in range(DEPTH):
        w_main, w_small, w_small_t = _split_in_weights(w_in[l])
        main, small, small_t = _inproj(hn, w_main, w_small, w_small_t)

        bias_i, bias_f = _gates_by_direction_head(mlstm_gate_b[l].astype(F32), axis=0)
        gb_row = jnp.broadcast_to(jnp.concatenate([bias_i, bias_f])[:, None], (N_GATES, CHUNK))
        gb_col = jnp.concatenate([bias_i, jnp.zeros((LANES - 2 * HEADS,), F32),
                                  bias_f, jnp.zeros((SMALL_W - LANES - 2 * HEADS,), F32)]).reshape(1, SMALL_W)
        w2 = jnp.zeros((2, LANES, QK_W), F32)
        w2 = w2.at[0, LR_OFF:LR_OFF + GLA_RANK].set(gla_w2[l, 0])
        w2 = w2.at[1, LR_OFF + GLA_RANK:LR_OFF + 2 * GLA_RANK].set(gla_w2[l, 1])
        y_a, y_b = _mixers(main, small, small_t, gb_row, gb_col,
                           mlstm_conv_w[l].astype(F32), row(mlstm_conv_b[l]), row(mlstm_norm[l]),
                           w2.astype(BF16), gla_b[l].reshape(2, 1, QK_W).astype(F32), row(gla_norm[l]))

        x2d, hn = _outproj(y_a, y_b, x2d, w_out[l].astype(BF16), row(norm_mix_post[l]), row(norm_ffn_pre[l]))

        h = _ffn_up(hn, ffn_w_gate[l].astype(BF16), ffn_w_up[l].astype(BF16),
                    ffn_conv_w[l].astype(F32), row(ffn_conv_b[l]))
        g_next = row(norm_mix_pre[l + 1]) if l + 1 < DEPTH else None
        x2d, hn = _ffn_down(h, x2d, ffn_w_down[l].astype(BF16), row(norm_ffn_post[l]), g_next)
    return x2d.reshape(BATCH, SEQ, D_MODEL)
```

```python
import functools

import jax
import jax.numpy as jnp
from jax import lax
from jax.experimental import pallas as pl
from jax.experimental.pallas import tpu as pltpu

F32 = jnp.float32
BF16 = jnp.bfloat16

D_MODEL = 1024
BATCH = 8
SEQ = 2048
DEPTH = 4
TOKENS = BATCH * SEQ
HEADS = 4
DK = 64
DV = 128
QK_W = HEADS * DK
V_W = HEADS * DV
GLA_RANK = 16
GLA_GATE_TEMP = 16.0
D_FF = 2816
EPS = 1e-6
IN_WIDTHS = (2 * QK_W, V_W, V_W, 4 * HEADS, QK_W, QK_W, V_W, V_W, 2 * GLA_RANK)
N_MAIN = 2 * QK_W + V_W + V_W + QK_W + QK_W + V_W + V_W
N_GATES = 4 * HEADS
SMALL_W = 256
LR_OFF = 16

LANES = 128
CHUNK = 128
NCHUNK = SEQ // CHUNK
ROW_BLK = 256
TM = 1024
EPI_RB = 256
FF_TN = 256
FF_BLOCKS = (512, 512, 512, 512)
assert sum(FF_BLOCKS) == SEQ
FF_ACT_RB = 128
GLA_LEVELS = (64, 32, 16, 8, 4, 2)
GLA_MILD_LOG_DECAY = 60.0
VMEM_LIMIT = 56 * 1024 * 1024


def _dot(a, b):
    return jnp.dot(a, b, preferred_element_type=F32)


def _dot_nt(a, b):
    return lax.dot_general(a, b, (((1,), (1,)), ((), ())), preferred_element_type=F32)


def _dot_tn(a, b):
    return lax.dot_general(a, b, (((0,), (0,)), ((), ())), preferred_element_type=F32)


def _scan(x, axis, reverse, op, identity):
    n = x.shape[axis]
    idx = lax.broadcasted_iota(jnp.int32, x.shape, axis)
    k = 1
    while k < n:
        if reverse:
            shifted = jnp.where(idx < n - k, pltpu.roll(x, n - k, axis), identity)
        else:
            shifted = jnp.where(idx >= k, pltpu.roll(x, k, axis), identity)
        x = op(x, shifted)
        k *= 2
    return x


def _split_bf16(x):
    hi = x.astype(BF16)
    r1 = x - hi.astype(F32)
    mid = r1.astype(BF16)
    lo = (r1 - mid.astype(F32)).astype(BF16)
    return hi, mid, lo


def _sel_dot(sel, x):
    hi, mid, lo = _split_bf16(x)
    return _dot(sel, hi) + _dot(sel, mid) + _dot(sel, lo)


def _dot_sel(x, sel):
    hi, mid, lo = _split_bf16(x)
    return _dot(hi, sel) + _dot(mid, sel) + _dot(lo, sel)


def _log_sigmoid(x):
    return jnp.minimum(x, 0.0) - jnp.log(1.0 + jnp.exp(-jnp.abs(x)))


def _sigmoid(x):
    return 1.0 / (1.0 + jnp.exp(-x))


def _rms_scale(x):
    return lax.rsqrt(jnp.mean(x * x, axis=-1, keepdims=True) + EPS)


def _rmsnorm_kernel(x_ref, g_ref, o_ref):
    x = x_ref[...]
    o_ref[...] = (x * _rms_scale(x) * g_ref[...]).astype(BF16)


def _rmsnorm(x2d, g):
    return pl.pallas_call(
        _rmsnorm_kernel,
        grid=(TOKENS // TM,),
        in_specs=[
            pl.BlockSpec((TM, D_MODEL), lambda i: (i, 0)),
            pl.BlockSpec((1, D_MODEL), lambda i: (0, 0)),
        ],
        out_specs=pl.BlockSpec((TM, D_MODEL), lambda i: (i, 0)),
        out_shape=jax.ShapeDtypeStruct((TOKENS, D_MODEL), BF16),
        compiler_params=pltpu.CompilerParams(
            dimension_semantics=("arbitrary",), vmem_limit_bytes=VMEM_LIMIT),
        name="rmsnorm",
    )(x2d, g)


def _inproj_kernel(hn_ref, wm_ref, ws_ref, wst_ref, main_ref, small_ref, smallt_ref):
    hn = hn_ref[...]
    for j in range(N_MAIN // 512):
        cols = slice(j * 512, (j + 1) * 512)
        main_ref[:, cols] = _dot(hn, wm_ref[:, cols]).astype(BF16)
    small_ref[...] = _dot(hn, ws_ref[...])
    st = _dot_nt(wst_ref[...], hn)
    for j in range(TM // CHUNK):
        smallt_ref[j] = st[:, j * CHUNK:(j + 1) * CHUNK]


def _inproj(hn, w_main, w_small, w_small_t):
    return pl.pallas_call(
        _inproj_kernel,
        grid=(TOKENS // TM,),
        in_specs=[
            pl.BlockSpec((TM, D_MODEL), lambda i: (i, 0)),
            pl.BlockSpec((D_MODEL, N_MAIN), lambda i: (0, 0)),
            pl.BlockSpec((D_MODEL, SMALL_W), lambda i: (0, 0)),
            pl.BlockSpec((N_GATES, D_MODEL), lambda i: (0, 0)),
        ],
        out_specs=[
            pl.BlockSpec((TM, N_MAIN), lambda i: (i, 0)),
            pl.BlockSpec((TM, SMALL_W), lambda i: (i, 0)),
            pl.BlockSpec((TM // CHUNK, N_GATES, CHUNK), lambda i: (i, 0, 0)),
        ],
        out_shape=[
            jax.ShapeDtypeStruct((TOKENS, N_MAIN), BF16),
            jax.ShapeDtypeStruct((TOKENS, SMALL_W), F32),
            jax.ShapeDtypeStruct((TOKENS // CHUNK, N_GATES, CHUNK), F32),
        ],
        compiler_params=pltpu.CompilerParams(
            dimension_semantics=("arbitrary",), vmem_limit_bytes=VMEM_LIMIT),
        name="inproj",
    )(hn, w_main, w_small, w_small_t)


def _tri_masks():
    t = lax.broadcasted_iota(jnp.int32, (CHUNK, CHUNK), 0)
    s = lax.broadcasted_iota(jnp.int32, (CHUNK, CHUNK), 1)
    return t, s


def _blk(idx, m):
    return jnp.right_shift(idx, m.bit_length() - 1)


def _half_mask(hh):
    lane = lax.broadcasted_iota(jnp.int32, (CHUNK, LANES), 1)
    return (lane < DK) if hh == 0 else (lane >= DK)


def _accumulate_or_finish(part, sum_sc, rows, pair, finish, gate_ref, nw_ref, y_ref, gate_fn):
    cols2 = slice(pair * 2 * DV, (pair + 1) * 2 * DV)
    if not finish:
        sum_sc[rows, cols2] = part
        return
    total = sum_sc[rows, cols2] + part
    for hh in range(2):
        cols = slice(pair * 2 * DV + hh * DV, pair * 2 * DV + (hh + 1) * DV)
        hs = total[:, hh * DV:(hh + 1) * DV]
        yn = hs * _rms_scale(hs) * nw_ref[:, cols]
        gate = gate_ref[rows, cols].astype(F32)
        y_ref[rows, cols] = (gate_fn(gate) * yn).astype(BF16)


def _mlstm_parts(aqk_ref, av_ref, ao_ref, gcol_ref, grow_ref, gbrow_ref, gbcol_ref, cw_ref, cb_ref, nw_ref,
                 y_ref, q_sc, k_sc, hsum_sc, ct_sc):

    def conv_body(r, carry):
        r0 = pl.multiple_of(r * ROW_BLK, ROW_BLK)
        cur = aqk_ref[pl.ds(r0, ROW_BLK), :].astype(F32)
        p0 = pl.multiple_of(jnp.maximum(r0 - 16, 0), 16)
        n0 = pl.multiple_of(jnp.minimum(r0 + ROW_BLK, SEQ - 16), 16)
        prev_row = aqk_ref[pl.ds(p0, 16), :].astype(F32)[15:16]
        next_row = aqk_ref[pl.ds(n0, 16), :].astype(F32)[0:1]
        prev_row = jnp.where(r == 0, 0.0, prev_row)
        next_row = jnp.where(r == SEQ // ROW_BLK - 1, 0.0, next_row)
        row = lax.broadcasted_iota(jnp.int32, (ROW_BLK, 1), 0)
        prev = jnp.where(row == 0, prev_row, pltpu.roll(cur, 1, 0))
        nxt = jnp.where(row == ROW_BLK - 1, next_row, pltpu.roll(cur, ROW_BLK - 1, 0))
        c = prev * cw_ref[0:1, :] + cur * cw_ref[1:2, :] + nxt * cw_ref[2:3, :] + cb_ref[...]
        act = c * _sigmoid(c)
        q_sc[pl.ds(r0, ROW_BLK), :] = act[:, :QK_W].astype(BF16)
        k_sc[pl.ds(r0, ROW_BLK), :] = (act[:, QK_W:] * (DK ** -0.5)).astype(BF16)
        return carry
    lax.fori_loop(0, SEQ // ROW_BLK, conv_body, 0)

    ct_sc[...] = jnp.zeros_like(ct_sc)

    t_idx, s_idx = _tri_masks()
    le = t_idx >= s_idx
    ge = t_idx <= s_idx
    tri_le = le.astype(BF16)
    tri_ge = ge.astype(BF16)
    lane_row = lax.broadcasted_iota(jnp.int32, (1, LANES), 1)
    half0 = _half_mask(0)
    half0_row = lane_row < DK
    lane2 = lax.broadcasted_iota(jnp.int32, (CHUNK, 2 * DV), 1)
    left_half = lane2 < DV
    bd_r = lax.broadcasted_iota(jnp.int32, (2 * DV, 2 * DV), 0) < DV
    bd_l = lax.broadcasted_iota(jnp.int32, (2 * DV, 2 * DV), 1) < DV
    ones_bd = (bd_r == bd_l).astype(BF16)
    st_r = lax.broadcasted_iota(jnp.int32, (2 * DV, LANES), 0) < DV
    st_l = lax.broadcasted_iota(jnp.int32, (2 * DV, LANES), 1) < DK
    state_mask = st_r == st_l

    def lane_bc(x, j):
        return jnp.broadcast_to(x[:, j:j + 1], (x.shape[0], LANES))

    def one_dir(c, direction, m_vec, n_rows, finish):
        r0 = pl.multiple_of(c * CHUNK, CHUNK)
        rows = pl.ds(r0, CHUNK)
        mask = le if direction == 0 else ge
        t_last = CHUNK - 1 if direction == 0 else 0

        g_row = grow_ref[c] + gbrow_ref[...]
        lf_row = _log_sigmoid(g_row[2 * HEADS:, :])
        b_row = _dot_sel(lf_row, tri_ge if direction == 0 else tri_le)
        a_row = g_row[:2 * HEADS, :] - b_row
        g_col = gcol_ref[rows, :] + gbcol_ref[...]
        lf_col = _log_sigmoid(g_col[:, LANES:])
        b_col = _sel_dot(tri_le if direction == 0 else tri_ge, lf_col)
        a_col = g_col[:, :LANES] - b_col
        pmax = _scan(a_col, 0, direction == 1, jnp.maximum, -jnp.inf)
        u = jnp.maximum(m_vec, pmax)
        u_last = u[t_last:t_last + 1, :]
        m_new = b_col[t_last:t_last + 1, :] + u_last
        thr = jnp.exp(-b_col - u)
        w = jnp.exp(a_col - u_last)
        decay = jnp.exp(m_vec - u_last)

        n_new = []
        for p in range(HEADS // 2):
            lanes = slice(p * LANES, (p + 1) * LANES)
            hd = [direction * HEADS + 2 * p, direction * HEADS + 2 * p + 1]
            q2 = q_sc[rows, lanes]
            k2 = k_sc[rows, lanes]
            u_bc = [lane_bc(u, j) for j in hd]
            e_pair = jnp.concatenate(
                [jnp.where(mask, jnp.exp(jnp.broadcast_to(a_row[j:j + 1, :], (CHUNK, CHUNK)) - ub), 0.0)
                 for j, ub in zip(hd, u_bc)], axis=1)
            zeros_k = jnp.zeros_like(k2)
            kb = jnp.concatenate([jnp.where(half0, k2, zeros_k), jnp.where(half0, zeros_k, k2)], axis=0)
            pm = (_dot_nt(q2, kb) * e_pair).astype(BF16)
            v_pair = av_ref[rows, p * 2 * DV:(p + 1) * 2 * DV]
            zeros_v = jnp.zeros_like(v_pair)
            v_bd = jnp.concatenate([jnp.where(left_half, v_pair, zeros_v),
                                    jnp.where(left_half, zeros_v, v_pair)], axis=0)
            inter = jnp.where(half0, jnp.exp(lane_bc(m_vec, hd[0]) - u_bc[0]),
                              jnp.exp(lane_bc(m_vec, hd[1]) - u_bc[1]))
            qi = (q2.astype(F32) * inter).astype(BF16)
            ct = ct_sc[direction, p]
            n_row = n_rows[direction * (HEADS // 2) + p]
            n_bd = jnp.where(state_mask, jnp.broadcast_to(n_row, (2 * DV, LANES)), 0.0).astype(BF16)
            num = _dot(pm, v_bd) + _dot_nt(qi, ct.astype(BF16))
            den = _dot(pm, ones_bd) + _dot_nt(qi, n_bd)
            thr_pair = jnp.concatenate([lane_bc(thr, hd[0]), lane_bc(thr, hd[1])], axis=1)
            _accumulate_or_finish(num / jnp.maximum(jnp.abs(den), thr_pair), hsum_sc, rows, p, finish,
                                  ao_ref, nw_ref, y_ref, _sigmoid)
            w_pair = jnp.where(half0, lane_bc(w, hd[0]), lane_bc(w, hd[1]))
            wk = k2.astype(F32) * w_pair
            decay_pair = jnp.where(half0_row, lane_bc(decay, hd[0]), lane_bc(decay, hd[1]))
            upd_t = _dot_tn(v_pair, wk.astype(BF16))
            ct_sc[direction, p] = ct * decay_pair + jnp.where(state_mask, upd_t, 0.0)
            n_new.append(n_row * decay_pair + jnp.sum(wk, axis=0, keepdims=True))
        return m_new, n_new

    def chunk_step(i, carry, finish):
        m_vec, n_rows = carry[0], carry[1:]
        m_fw, n_fw = one_dir(i, 0, m_vec, n_rows, finish)
        m_bw, n_bw = one_dir(NCHUNK - 1 - i, 1, m_vec, n_rows, finish)
        m_next = jnp.where(lane_row < HEADS, m_fw, m_bw)
        return (m_next,) + tuple(n_fw) + tuple(n_bw)

    init = tuple(jnp.zeros((1, LANES), F32) for _ in range(1 + 2 * (HEADS // 2)))
    return init, chunk_step


def _gla_parts(bqk_ref, bv_ref, br_ref, gcol_ref, w2_ref, gb_ref, nw_ref,
               y_ref, la_sc, osum_sc, st_sc):

    def pre_body(r, tot_min):
        rows = pl.ds(pl.multiple_of(r * ROW_BLK, ROW_BLK), ROW_BLK)
        lr = gcol_ref[rows, :LANES].astype(BF16)
        for d in range(2):
            z = _dot(lr, w2_ref[d]) + gb_ref[d]
            la = _log_sigmoid(z) * (1.0 / GLA_GATE_TEMP)
            la_sc[d, rows, :] = la
            for cc in range(ROW_BLK // CHUNK):
                chunk_total = jnp.sum(la[cc * CHUNK:(cc + 1) * CHUNK, :], axis=0, keepdims=True)
                tot_min = jnp.minimum(tot_min, chunk_total)
        return tot_min
    tot_min = lax.fori_loop(0, SEQ // ROW_BLK, pre_body, jnp.zeros((1, QK_W), F32))
    all_mild = jnp.min(tot_min) >= -GLA_MILD_LOG_DECAY

    st_sc[...] = jnp.zeros_like(st_sc)

    t_idx, s_idx = _tri_masks()
    eye = t_idx == s_idx
    lane2 = lax.broadcasted_iota(jnp.int32, (CHUNK, 2 * DV), 1)
    left_half = lane2 < DV
    bd_rows = lax.broadcasted_iota(jnp.int32, (2 * DV, LANES), 0) < DV
    bd_lanes = lax.broadcasted_iota(jnp.int32, (2 * DV, LANES), 1) < DK
    state_mask = bd_rows == bd_lanes

    def intra_mild(q2, k2, b_p, causal):
        qf = (q2 * jnp.exp(b_p)).astype(BF16)
        kf = (k2 * jnp.exp(-b_p)).astype(BF16)
        kb = jnp.concatenate([jnp.where(_half_mask(0), kf, jnp.zeros_like(kf)),
                              jnp.where(_half_mask(1), kf, jnp.zeros_like(kf))], axis=0)
        a_pair = _dot_nt(qf, kb)
        causal2 = jnp.concatenate([causal, causal], axis=1)
        return jnp.where(causal2, a_pair, 0.0)

    def intra_robust(q2, k2, la_p, seg_sum_p, direction, causal, strict_rev):
        k2b = k2.astype(BF16)
        acc = []
        for hh in range(2):
            qm = jnp.where(_half_mask(hh), q2, 0.0).astype(BF16)
            acc.append(jnp.where(eye, _dot_nt(qm, k2b), 0.0))
        for m in GLA_LEVELS + (1,):
            if m == 1:
                dq, dk = la_p, None
            else:
                same = _blk(t_idx, m) == _blk(s_idx, m)
                dq = seg_sum_p(same & causal)
                dk = seg_sum_p(same & strict_rev)
            qt = q2 * jnp.exp(dq)
            ktb = (k2 if dk is None else k2 * jnp.exp(dk)).astype(BF16)
            tb = _blk(t_idx, m)
            sb = _blk(s_idx, m)
            if direction == 0:
                lvl_mask = ((tb & 1) == 1) & (sb == tb - 1)
            else:
                lvl_mask = ((tb & 1) == 0) & (sb == tb + 1)
            for hh in range(2):
                qm = jnp.where(_half_mask(hh), qt, 0.0).astype(BF16)
                acc[hh] = acc[hh] + jnp.where(lvl_mask, _dot_nt(qm, ktb), 0.0)
        return jnp.concatenate(acc, axis=1)

    def one_dir(c, direction, mild, finish):
        r0 = pl.multiple_of(c * CHUNK, CHUNK)
        rows = pl.ds(r0, CHUNK)
        causal = (t_idx >= s_idx) if direction == 0 else (t_idx <= s_idx)
        strict_rev = (s_idx > t_idx) if direction == 0 else (s_idx < t_idx)
        t_last = CHUNK - 1 if direction == 0 else 0

        la = la_sc[direction, rows, :]
        b = _scan(la, 0, direction == 1, jnp.add, 0.0)
        b_tot = b[t_last:t_last + 1, :]
        eb = jnp.exp(b)
        k_decay = jnp.exp(b_tot - b)
        state_decay = jnp.exp(b_tot)

        for p in range(HEADS // 2):
            lanes = slice(p * LANES, (p + 1) * LANES)
            q2 = bqk_ref[rows, lanes].astype(F32) * (DK ** -0.5)
            k2 = bqk_ref[rows, QK_W + p * LANES:QK_W + (p + 1) * LANES].astype(F32)
            if mild:
                a_pair = intra_mild(q2, k2, b[:, lanes], causal)
            else:
                la_p3 = _split_bf16(la[:, lanes])

                def seg_sum_p(sel):
                    selb = sel.astype(BF16)
                    return _dot(selb, la_p3[0]) + _dot(selb, la_p3[1]) + _dot(selb, la_p3[2])
                a_pair = intra_robust(q2, k2, la[:, lanes], seg_sum_p, direction, causal, strict_rev)
            v_pair = bv_ref[rows, p * 2 * DV:(p + 1) * 2 * DV]
            zeros_v = jnp.zeros_like(v_pair)
            v_bd = jnp.concatenate([jnp.where(left_half, v_pair, zeros_v),
                                    jnp.where(left_half, zeros_v, v_pair)], axis=0)
            st = st_sc[direction, p]
            q_inter = (q2 * eb[:, lanes]).astype(BF16)
            o = _dot(a_pair.astype(BF16), v_bd) + _dot_nt(q_inter, st.astype(BF16))
            _accumulate_or_finish(o, osum_sc, rows, p, finish, br_ref, nw_ref, y_ref, lambda g: g * _sigmoid(g))
            k_state = (k2 * k_decay[:, lanes]).astype(BF16)
            upd_t = _dot_tn(v_pair, k_state)
            st_sc[direction, p] = st * state_decay[:, lanes] + jnp.where(state_mask, upd_t, 0.0)

    def chunk_step(i, mild, finish):
        one_dir(i, 0, mild, finish)
        one_dir(NCHUNK - 1 - i, 1, mild, finish)

    return all_mild, chunk_step


def _mixers_kernel(aqk_ref, av_ref, ao_ref, bqk_ref, bv_ref, br_ref, gcol_ref, grow_ref, gbrow_ref, gbcol_ref,
                   cw_ref, cb_ref, nwa_ref, w2_ref, gb_ref, nwb_ref, ya_ref, yb_ref,
                   q_sc, k_sc, hsum_sc, ct_sc, la_sc, osum_sc, st_sc):
    m_init, m_step = _mlstm_parts(aqk_ref, av_ref, ao_ref, gcol_ref, grow_ref, gbrow_ref, gbcol_ref,
                                  cw_ref, cb_ref, nwa_ref, ya_ref, q_sc, k_sc, hsum_sc, ct_sc)
    all_mild, g_step = _gla_parts(bqk_ref, bv_ref, br_ref, gcol_ref, w2_ref, gb_ref, nwb_ref,
                                  yb_ref, la_sc, osum_sc, st_sc)

    def run(mild):
        def half(finish):
            def chunk_body(i, carry):
                g_step(i, mild, finish)
                return m_step(i, carry, finish)
            return chunk_body
        carry = lax.fori_loop(0, NCHUNK // 2, half(False), m_init, unroll=2 if mild else 1)
        lax.fori_loop(NCHUNK // 2, NCHUNK, half(True), carry)

    @pl.when(all_mild)
    def _():
        run(True)

    @pl.when(jnp.logical_not(all_mild))
    def _():
        run(False)


def _mixers(main, small, small_t, gb_row, gb_col, conv_w, conv_b, norm_a, w2, gb, norm_b):
    def blk(j):
        return pl.BlockSpec((SEQ, V_W), lambda b: (b, j))
    full = lambda shape: pl.BlockSpec(shape, lambda b: tuple(0 for _ in shape))
    return pl.pallas_call(
        _mixers_kernel,
        grid=(BATCH,),
        in_specs=[
            blk(0), blk(1), blk(2), blk(3), blk(4), blk(5),
            pl.BlockSpec((SEQ, SMALL_W), lambda b: (b, 0)),
            pl.BlockSpec((NCHUNK, N_GATES, CHUNK), lambda b: (b, 0, 0)),
            full((N_GATES, CHUNK)), full((1, SMALL_W)), full((3, 2 * QK_W)), full((1, 2 * QK_W)), full((1, V_W)),
            full((2, LANES, QK_W)), full((2, 1, QK_W)), full((1, V_W)),
        ],
        out_specs=[pl.BlockSpec((SEQ, V_W), lambda b: (b, 0)), pl.BlockSpec((SEQ, V_W), lambda b: (b, 0))],
        out_shape=[jax.ShapeDtypeStruct((TOKENS, V_W), BF16), jax.ShapeDtypeStruct((TOKENS, V_W), BF16)],
        scratch_shapes=[
            pltpu.VMEM((SEQ, QK_W), BF16),
            pltpu.VMEM((SEQ, QK_W), BF16),
            pltpu.VMEM((SEQ, V_W), F32),
            pltpu.VMEM((2, HEADS // 2, 2 * DV, 2 * DK), F32),
            pltpu.VMEM((2, SEQ, QK_W), F32),
            pltpu.VMEM((SEQ, V_W), F32),
            pltpu.VMEM((2, HEADS // 2, 2 * DV, 2 * DK), F32),
        ],
        compiler_params=pltpu.CompilerParams(
            dimension_semantics=("arbitrary",), vmem_limit_bytes=VMEM_LIMIT),
        name="mixers",
    )(main, main, main, main, main, main, small, small_t, gb_row, gb_col, conv_w, conv_b, norm_a, w2, gb, norm_b)


def _residual_and_next_norm(rows, x_ref, branch, g_post_ref, g_next_ref, o_ref, hn_ref):
    x_new = x_ref[rows, :] + branch * _rms_scale(branch) * g_post_ref[...]
    o_ref[rows, :] = x_new
    if hn_ref is not None:
        hn_ref[rows, :] = (x_new * _rms_scale(x_new) * g_next_ref[...]).astype(BF16)


def _row_blocks():
    return [slice(r, r + EPI_RB) for r in range(0, TM, EPI_RB)]


def _outproj_kernel(ya_ref, yb_ref, x_ref, w_ref, g_post_ref, g_next_ref, o_ref, hn_ref):
    for rows in _row_blocks():
        mix = _dot(ya_ref[rows, :], w_ref[:V_W, :]) + _dot(yb_ref[rows, :], w_ref[V_W:, :])
        _residual_and_next_norm(rows, x_ref, mix, g_post_ref, g_next_ref, o_ref, hn_ref)


def _row_tile(width):
    return pl.BlockSpec((TM, width), lambda i: (i, 0))


def _resident(shape):
    return pl.BlockSpec(shape, lambda i: (0, 0))


def _outproj(ya, yb, x2d, w, g_post, g_next):
    return pl.pallas_call(
        _outproj_kernel,
        grid=(TOKENS // TM,),
        in_specs=[_row_tile(V_W), _row_tile(V_W), _row_tile(D_MODEL), _resident((D_MODEL, D_MODEL)),
                  _resident((1, D_MODEL)), _resident((1, D_MODEL))],
        out_specs=[_row_tile(D_MODEL), _row_tile(D_MODEL)],
        out_shape=[jax.ShapeDtypeStruct((TOKENS, D_MODEL), F32),
                   jax.ShapeDtypeStruct((TOKENS, D_MODEL), BF16)],
        compiler_params=pltpu.CompilerParams(
            dimension_semantics=("arbitrary",), vmem_limit_bytes=VMEM_LIMIT),
        name="outproj",
    )(ya, yb, x2d, w, g_post, g_next)


def _gelu_tanh(x):
    k1 = -2.0 * 0.7978845608028654 * 1.4426950408889634
    return x / (1.0 + jnp.exp2(x * (k1 + (k1 * 0.044715) * (x * x))))


def _ffn_up_kernel(hn_ref, wg_ref, wu_ref, cw_ref, cb_ref, h_ref, g_sc, u_sc):
    g_sc[0:8, :] = jnp.zeros((8, FF_TN), F32)
    g_sc[SEQ + 8:SEQ + 16, :] = jnp.zeros((8, FF_TN), F32)
    n_blk = len(FF_BLOCKS)
    starts = [sum(FF_BLOCKS[:i]) for i in range(n_blk)]

    def project(i):
        rows = slice(starts[i], starts[i] + FF_BLOCKS[i])
        hn = hn_ref[rows, :]
        g_sc[8 + rows.start:8 + rows.stop, :] = _dot(hn, wg_ref[...])
        u_sc[rows, :] = _dot(hn, wu_ref[...])

    row = lax.broadcasted_iota(jnp.int32, (FF_ACT_RB, 1), 0)
    first_row = row == 0
    last_row = row == FF_ACT_RB - 1

    def activate(i):
        for j in range(FF_BLOCKS[i] // FF_ACT_RB):
            r0 = starts[i] + j * FF_ACT_RB
            cur = g_sc[r0 + 8:r0 + 8 + FF_ACT_RB, :]
            prev_row = g_sc[r0 + 7:r0 + 8, :]
            next_row = g_sc[r0 + 8 + FF_ACT_RB:r0 + 9 + FF_ACT_RB, :]
            prev = jnp.where(first_row, prev_row, pltpu.roll(cur, 1, 0))
            nxt = jnp.where(last_row, next_row, pltpu.roll(cur, FF_ACT_RB - 1, 0))
            gc = prev * cw_ref[0:1, :] + cur * cw_ref[1:2, :] + nxt * cw_ref[2:3, :] + cb_ref[...]
            h_ref[r0:r0 + FF_ACT_RB, :] = (_gelu_tanh(gc) * u_sc[r0:r0 + FF_ACT_RB, :]).astype(BF16)

    for i in range(n_blk + 1):
        if i < n_blk:
            project(i)
        if i >= 1:
            activate(i - 1)


def _ffn_up(hn, wg, wu, cw, cb):
    return pl.pallas_call(
        _ffn_up_kernel,
        grid=(BATCH, D_FF // FF_TN),
        in_specs=[
            pl.BlockSpec((SEQ, D_MODEL), lambda b, n: (b, 0)),
            pl.BlockSpec((D_MODEL, FF_TN), lambda b, n: (0, n)),
            pl.BlockSpec((D_MODEL, FF_TN), lambda b, n: (0, n)),
            pl.BlockSpec((3, FF_TN), lambda b, n: (0, n)),
            pl.BlockSpec((1, FF_TN), lambda b, n: (0, n)),
        ],
        out_specs=pl.BlockSpec((SEQ, FF_TN), lambda b, n: (b, n)),
        out_shape=jax.ShapeDtypeStruct((TOKENS, D_FF), BF16),
        scratch_shapes=[
            pltpu.VMEM((SEQ + 16, FF_TN), F32),
            pltpu.VMEM((SEQ, FF_TN), F32),
        ],
        compiler_params=pltpu.CompilerParams(
            dimension_semantics=("arbitrary", "arbitrary"), vmem_limit_bytes=VMEM_LIMIT),
        name="ffn_up",
    )(hn, wg, wu, cw, cb)


def _ffn_down_kernel(h_ref, x_ref, w_ref, g_post_ref, g_next_ref, o_ref, hn_ref):
    for rows in _row_blocks():
        ff = _dot(h_ref[rows, :], w_ref[...])
        _residual_and_next_norm(rows, x_ref, ff, g_post_ref, g_next_ref, o_ref, hn_ref)


def _ffn_down_last_kernel(h_ref, x_ref, w_ref, g_post_ref, o_ref):
    for rows in _row_blocks():
        ff = _dot(h_ref[rows, :], w_ref[...])
        _residual_and_next_norm(rows, x_ref, ff, g_post_ref, None, o_ref, None)


def _ffn_down(h, x2d, w, g_post, g_next):
    last = g_next is None
    in_specs = [_row_tile(D_FF), _row_tile(D_MODEL), _resident((D_FF, D_MODEL)), _resident((1, D_MODEL))]
    out_specs = [_row_tile(D_MODEL)]
    out_shape = [jax.ShapeDtypeStruct((TOKENS, D_MODEL), F32)]
    args = [h, x2d, w, g_post]
    if not last:
        in_specs.append(_resident((1, D_MODEL)))
        out_specs.append(_row_tile(D_MODEL))
        out_shape.append(jax.ShapeDtypeStruct((TOKENS, D_MODEL), BF16))
        args.append(g_next)
    outs = pl.pallas_call(
        _ffn_down_last_kernel if last else _ffn_down_kernel,
        grid=(TOKENS // TM,),
        in_specs=in_specs,
        out_specs=out_specs,
        out_shape=out_shape,
        compiler_params=pltpu.CompilerParams(
            dimension_semantics=("arbitrary",), vmem_limit_bytes=VMEM_LIMIT),
        name="ffn_down_last" if last else "ffn_down",
    )(*args)
    return (outs[0], None) if last else (outs[0], outs[1])


def _split_in_weights(w_in_l):
    offs = [0]
    for w in IN_WIDTHS:
        offs.append(offs[-1] + w)
    a_qk, a_v, a_o, gates, b_q, b_k, b_v, b_r, b_lr = (w_in_l[:, offs[i]:offs[i + 1]] for i in range(9))
    w_main = jnp.concatenate([a_qk, a_v, a_o, b_q, b_k, b_v, b_r], axis=1).astype(BF16)
    gates_i, gates_f = _gates_by_direction_head(gates, axis=1)
    pad = lambda n: jnp.zeros((D_MODEL, n), w_in_l.dtype)
    w_small = jnp.concatenate([gates_i, pad(LR_OFF - 2 * HEADS), b_lr, pad(LANES - LR_OFF - 2 * GLA_RANK),
                               gates_f, pad(SMALL_W - LANES - 2 * HEADS)], axis=1).astype(BF16)
    w_gates_t = jnp.concatenate([gates_i, gates_f], axis=1).T.astype(BF16)
    return w_main, w_small, w_gates_t


def _gates_by_direction_head(g, axis):
    i_fw, f_fw, i_bw, f_bw = jnp.split(g, 4, axis=axis)
    return jnp.concatenate([i_fw, i_bw], axis=axis), jnp.concatenate([f_fw, f_bw], axis=axis)


def kernel(x, norm_mix_pre, norm_mix_post, norm_ffn_pre, norm_ffn_post, w_in, mlstm_gate_b, mlstm_conv_w,
           mlstm_conv_b, mlstm_norm, gla_w2, gla_b, gla_norm, w_out, ffn_w_gate, ffn_w_up, ffn_conv_w,
           ffn_conv_b, ffn_w_down):
    def row(v):
        return v.reshape(1, -1).astype(F32)

    x2d = x.reshape(TOKENS, D_MODEL).astype(F32)
    hn = _rmsnorm(x2d, row(norm_mix_pre[0]))
    for l in range(DEPTH):
        w_main, w_small, w_small_t = _split_in_weights(w_in[l])
        main, small, small_t = _inproj(hn, w_main, w_small, w_small_t)

        bias_i, bias_f = _gates_by_direction_head(mlstm_gate_b[l].astype(F32), axis=0)
        gb_row = jnp.broadcast_to(jnp.concatenate([bias_i, bias_f])[:, None], (N_GATES, CHUNK))
        gb_col = jnp.concatenate([bias_i, jnp.zeros((LANES - 2 * HEADS,), F32),
                                  bias_f, jnp.zeros((SMALL_W - LANES - 2 * HEADS,), F32)]).reshape(1, SMALL_W)
        w2 = jnp.zeros((2, LANES, QK_W), F32)
        w2 = w2.at[0, LR_OFF:LR_OFF + GLA_RANK].set(gla_w2[l, 0])
        w2 = w2.at[1, LR_OFF + GLA_RANK:LR_OFF + 2 * GLA_RANK].set(gla_w2[l, 1])
        y_a, y_b = _mixers(main, small, small_t, gb_row, gb_col,
                           mlstm_conv_w[l].astype(F32), row(mlstm_conv_b[l]), row(mlstm_norm[l]),
                           w2.astype(BF16), gla_b[l].reshape(2, 1, QK_W).astype(F32), row(gla_norm[l]))

        x2d, hn = _outproj(y_a, y_b, x2d, w_out[l].astype(BF16), row(norm_mix_post[l]), row(norm_ffn_pre[l]))

        h = _ffn_up(hn, ffn_w_gate[l].astype(BF16), ffn_w_up[l].astype(BF16),
                    ffn_conv_w[l].astype(F32), row(ffn_conv_b[l]))
        g_next = row(norm_mix_pre[l + 1]) if l + 1 < DEPTH else None
        x2d, hn = _ffn_down(h, x2d, ffn_w_down[l].astype(BF16), row(norm_ffn_post[l]), g_next)
    return x2d.reshape(BATCH, SEQ, D_MODEL)
```

```python
import functools

import jax
import jax.numpy as jnp
from jax import lax
from jax.experimental import pallas as pl
from jax.experimental.pallas import tpu as pltpu

F32 = jnp.float32
BF16 = jnp.bfloat16

D_MODEL = 1024
BATCH = 8
SEQ = 2048
DEPTH = 4
TOKENS = BATCH * SEQ
HEADS = 4
DK = 64
DV = 128
QK_W = HEADS * DK
V_W = HEADS * DV
GLA_RANK = 16
GLA_GATE_TEMP = 16.0
D_FF = 2816
EPS = 1e-6
IN_WIDTHS = (2 * QK_W, V_W, V_W, 4 * HEADS, QK_W, QK_W, V_W, V_W, 2 * GLA_RANK)
N_MAIN = 2 * QK_W + V_W + V_W + QK_W + QK_W + V_W + V_W
N_GATES = 4 * HEADS
SMALL_W = 256
LR_OFF = 16

LANES = 128
CHUNK = 128
NCHUNK = SEQ // CHUNK
ROW_BLK = 256
TM = 1024
EPI_RB = 256
FF_TN = 256
FF_BLOCKS = (512, 512, 512, 512)
assert sum(FF_BLOCKS) == SEQ
FF_ACT_RB = 128
GLA_LEVELS = (64, 32, 16, 8, 4, 2)
GLA_MILD_LOG_DECAY = 60.0
VMEM_LIMIT = 56 * 1024 * 1024


def _dot(a, b):
    return jnp.dot(a, b, preferred_element_type=F32)


def _dot_nt(a, b):
    return lax.dot_general(a, b, (((1,), (1,)), ((), ())), preferred_element_type=F32)


def _dot_tn(a, b):
    return lax.dot_general(a, b, (((0,), (0,)), ((), ())), preferred_element_type=F32)


def _scan(x, axis, reverse, op, identity):
    n = x.shape[axis]
    idx = lax.broadcasted_iota(jnp.int32, x.shape, axis)
    k = 1
    while k < n:
        if reverse:
            shifted = jnp.where(idx < n - k, pltpu.roll(x, n - k, axis), identity)
        else:
            shifted = jnp.where(idx >= k, pltpu.roll(x, k, axis), identity)
        x = op(x, shifted)
        k *= 2
    return x


def _split_bf16(x):
    hi = x.astype(BF16)
    r1 = x - hi.astype(F32)
    mid = r1.astype(BF16)
    lo = (r1 - mid.astype(F32)).astype(BF16)
    return hi, mid, lo


def _sel_dot(sel, x):
    hi, mid, lo = _split_bf16(x)
    return _dot(sel, hi) + _dot(sel, mid) + _dot(sel, lo)


def _dot_sel(x, sel):
    hi, mid, lo = _split_bf16(x)
    return _dot(hi, sel) + _dot(mid, sel) + _dot(lo, sel)


def _log_sigmoid(x):
    return jnp.minimum(x, 0.0) - jnp.log(1.0 + jnp.exp(-jnp.abs(x)))


def _sigmoid(x):
    return 1.0 / (1.0 + jnp.exp(-x))


def _rms_scale(x):
    return lax.rsqrt(jnp.mean(x * x, axis=-1, keepdims=True) + EPS)


def _rmsnorm_kernel(x_ref, g_ref, o_ref):
    x = x_ref[...]
    o_ref[...] = (x * _rms_scale(x) * g_ref[...]).astype(BF16)


def _rmsnorm(x2d, g):
    return pl.pallas_call(
        _rmsnorm_kernel,
        grid=(TOKENS // TM,),
        in_specs=[
            pl.BlockSpec((TM, D_MODEL), lambda i: (i, 0)),
            pl.BlockSpec((1, D_MODEL), lambda i: (0, 0)),
        ],
        out_specs=pl.BlockSpec((TM, D_MODEL), lambda i: (i, 0)),
        out_shape=jax.ShapeDtypeStruct((TOKENS, D_MODEL), BF16),
        compiler_params=pltpu.CompilerParams(
            dimension_semantics=("arbitrary",), vmem_limit_bytes=VMEM_LIMIT),
        name="rmsnorm",
    )(x2d, g)


def _inproj_kernel(hn_ref, wm_ref, ws_ref, wst_ref, main_ref, small_ref, smallt_ref):
    hn = hn_ref[...]
    for j in range(N_MAIN // 512):
        cols = slice(j * 512, (j + 1) * 512)
        main_ref[:, cols] = _dot(hn, wm_ref[:, cols]).astype(BF16)
    small_ref[...] = _dot(hn, ws_ref[...])
    st = _dot_nt(wst_ref[...], hn)
    for j in range(TM // CHUNK):
        smallt_ref[j] = st[:, j * CHUNK:(j + 1) * CHUNK]


def _inproj(hn, w_main, w_small, w_small_t):
    return pl.pallas_call(
        _inproj_kernel,
        grid=(TOKENS // TM,),
        in_specs=[
            pl.BlockSpec((TM, D_MODEL), lambda i: (i, 0)),
            pl.BlockSpec((D_MODEL, N_MAIN), lambda i: (0, 0)),
            pl.BlockSpec((D_MODEL, SMALL_W), lambda i: (0, 0)),
            pl.BlockSpec((N_GATES, D_MODEL), lambda i: (0, 0)),
        ],
        out_specs=[
            pl.BlockSpec((TM, N_MAIN), lambda i: (i, 0)),
            pl.BlockSpec((TM, SMALL_W), lambda i: (i, 0)),
            pl.BlockSpec((TM // CHUNK, N_GATES, CHUNK), lambda i: (i, 0, 0)),
        ],
        out_shape=[
            jax.ShapeDtypeStruct((TOKENS, N_MAIN), BF16),
            jax.ShapeDtypeStruct((TOKENS, SMALL_W), F32),
            jax.ShapeDtypeStruct((TOKENS // CHUNK, N_GATES, CHUNK), F32),
        ],
        compiler_params=pltpu.CompilerParams(
            dimension_semantics=("arbitrary",), vmem_limit_bytes=VMEM_LIMIT),
        name="inproj",
    )(hn, w_main, w_small, w_small_t)


def _tri_masks():
    t = lax.broadcasted_iota(jnp.int32, (CHUNK, CHUNK), 0)
    s = lax.broadcasted_iota(jnp.int32, (CHUNK, CHUNK), 1)
    return t, s


def _blk(idx, m):
    return jnp.right_shift(idx, m.bit_length() - 1)


def _half_mask(hh):
    lane = lax.broadcasted_iota(jnp.int32, (CHUNK, LANES), 1)
    return (lane < DK) if hh == 0 else (lane >= DK)


def _accumulate_or_finish(part, sum_sc, rows, pair, finish, gate_ref, nw_ref, y_ref, gate_fn):
    cols2 = slice(pair * 2 * DV, (pair + 1) * 2 * DV)
    if not finish:
        sum_sc[rows, cols2] = part
        return
    total = sum_sc[rows, cols2] + part
    for hh in range(2):
        cols = slice(pair * 2 * DV + hh * DV, pair * 2 * DV + (hh + 1) * DV)
        hs = total[:, hh * DV:(hh + 1) * DV]
        yn = hs * _rms_scale(hs) * nw_ref[:, cols]
        gate = gate_ref[rows, cols].astype(F32)
        y_ref[rows, cols] = (gate_fn(gate) * yn).astype(BF16)


def _mlstm_parts(aqk_ref, av_ref, ao_ref, gcol_ref, grow_ref, gbrow_ref, gbcol_ref, cw_ref, cb_ref, nw_ref,
                 y_ref, q_sc, k_sc, hsum_sc, ct_sc):

    def conv_chunk(c):
        r0 = pl.multiple_of(c * CHUNK, CHUNK)
        cur = aqk_ref[pl.ds(r0, CHUNK), :].astype(F32)
        p0 = pl.multiple_of(jnp.maximum(r0 - 16, 0), 16)
        n0 = pl.multiple_of(jnp.minimum(r0 + CHUNK, SEQ - 16), 16)
        prev_row = aqk_ref[pl.ds(p0, 16), :].astype(F32)[15:16]
        next_row = aqk_ref[pl.ds(n0, 16), :].astype(F32)[0:1]
        prev_row = jnp.where(c == 0, 0.0, prev_row)
        next_row = jnp.where(c == NCHUNK - 1, 0.0, next_row)
        row = lax.broadcasted_iota(jnp.int32, (CHUNK, 1), 0)
        prev = jnp.where(row == 0, prev_row, pltpu.roll(cur, 1, 0))
        nxt = jnp.where(row == CHUNK - 1, next_row, pltpu.roll(cur, CHUNK - 1, 0))
        z = prev * cw_ref[0:1, :] + cur * cw_ref[1:2, :] + nxt * cw_ref[2:3, :] + cb_ref[...]
        act = z * _sigmoid(z)
        return act[:, :QK_W].astype(BF16), (act[:, QK_W:] * (DK ** -0.5)).astype(BF16)

    ct_sc[...] = jnp.zeros_like(ct_sc)

    t_idx, s_idx = _tri_masks()
    le = t_idx >= s_idx
    ge = t_idx <= s_idx
    tri_le = le.astype(BF16)
    tri_ge = ge.astype(BF16)
    lane_row = lax.broadcasted_iota(jnp.int32, (1, LANES), 1)
    half0 = _half_mask(0)
    half0_row = lane_row < DK
    lane2 = lax.broadcasted_iota(jnp.int32, (CHUNK, 2 * DV), 1)
    left_half = lane2 < DV
    bd_r = lax.broadcasted_iota(jnp.int32, (2 * DV, 2 * DV), 0) < DV
    bd_l = lax.broadcasted_iota(jnp.int32, (2 * DV, 2 * DV), 1) < DV
    ones_bd = (bd_r == bd_l).astype(BF16)
    st_r = lax.broadcasted_iota(jnp.int32, (2 * DV, LANES), 0) < DV
    st_l = lax.broadcasted_iota(jnp.int32, (2 * DV, LANES), 1) < DK
    state_mask = st_r == st_l

    def lane_bc(x, j):
        return jnp.broadcast_to(x[:, j:j + 1], (x.shape[0], LANES))

    def one_dir(c, direction, m_vec, n_rows, finish):
        r0 = pl.multiple_of(c * CHUNK, CHUNK)
        rows = pl.ds(r0, CHUNK)
        mask = le if direction == 0 else ge
        t_last = CHUNK - 1 if direction == 0 else 0

        g_row = grow_ref[c] + gbrow_ref[...]
        lf_row = _log_sigmoid(g_row[2 * HEADS:, :])
        b_row = _dot_sel(lf_row, tri_ge if direction == 0 else tri_le)
        a_row = g_row[:2 * HEADS, :] - b_row
        g_col = gcol_ref[rows, :] + gbcol_ref[...]
        lf_col = _log_sigmoid(g_col[:, LANES:])
        b_col = _sel_dot(tri_le if direction == 0 else tri_ge, lf_col)
        a_col = g_col[:, :LANES] - b_col
        pmax = _scan(a_col, 0, direction == 1, jnp.maximum, -jnp.inf)
        u = jnp.maximum(m_vec, pmax)
        u_last = u[t_last:t_last + 1, :]
        m_new = b_col[t_last:t_last + 1, :] + u_last
        thr = jnp.exp(-b_col - u)
        w = jnp.exp(a_col - u_last)
        decay = jnp.exp(m_vec - u_last)

        if not finish:
            q_all, k_all = conv_chunk(c)
            q_sc[rows, :] = q_all
            k_sc[rows, :] = k_all

        n_new = []
        for p in range(HEADS // 2):
            lanes = slice(p * LANES, (p + 1) * LANES)
            hd = [direction * HEADS + 2 * p, direction * HEADS + 2 * p + 1]
            if finish:
                q2 = q_sc[rows, lanes]
                k2 = k_sc[rows, lanes]
            else:
                q2 = q_all[:, lanes]
                k2 = k_all[:, lanes]
            u_bc = [lane_bc(u, j) for j in hd]
            e_pair = jnp.concatenate(
                [jnp.where(mask, jnp.exp(jnp.broadcast_to(a_row[j:j + 1, :], (CHUNK, CHUNK)) - ub), 0.0)
                 for j, ub in zip(hd, u_bc)], axis=1)
            zeros_k = jnp.zeros_like(k2)
            kb = jnp.concatenate([jnp.where(half0, k2, zeros_k), jnp.where(half0, zeros_k, k2)], axis=0)
            pm = (_dot_nt(q2, kb) * e_pair).astype(BF16)
            v_pair = av_ref[rows, p * 2 * DV:(p + 1) * 2 * DV]
            zeros_v = jnp.zeros_like(v_pair)
            v_bd = jnp.concatenate([jnp.where(left_half, v_pair, zeros_v),
                                    jnp.where(left_half, zeros_v, v_pair)], axis=0)
            inter = jnp.where(half0, jnp.exp(lane_bc(m_vec, hd[0]) - u_bc[0]),
                              jnp.exp(lane_bc(m_vec, hd[1]) - u_bc[1]))
            qi = (q2.astype(F32) * inter).astype(BF16)
            ct = ct_sc[direction, p]
            n_row = n_rows[direction * (HEADS // 2) + p]
            n_bd = jnp.where(state_mask, jnp.broadcast_to(n_row, (2 * DV, LANES)), 0.0).astype(BF16)
            num = _dot(pm, v_bd) + _dot_nt(qi, ct.astype(BF16))
            den = _dot(pm, ones_bd) + _dot_nt(qi, n_bd)
            thr_pair = jnp.concatenate([lane_bc(thr, hd[0]), lane_bc(thr, hd[1])], axis=1)
            _accumulate_or_finish(num / jnp.maximum(jnp.abs(den), thr_pair), hsum_sc, rows, p, finish,
                                  ao_ref, nw_ref, y_ref, _sigmoid)
            w_pair = jnp.where(half0, lane_bc(w, hd[0]), lane_bc(w, hd[1]))
            wk = k2.astype(F32) * w_pair
            decay_pair = jnp.where(half0_row, lane_bc(decay, hd[0]), lane_bc(decay, hd[1]))
            upd_t = _dot_tn(v_pair, wk.astype(BF16))
            ct_sc[direction, p] = ct * decay_pair + jnp.where(state_mask, upd_t, 0.0)
            n_new.append(n_row * decay_pair + jnp.sum(wk, axis=0, keepdims=True))
        return m_new, n_new

    def chunk_step(i, carry, finish):
        m_vec, n_rows = carry[0], carry[1:]
        m_fw, n_fw = one_dir(i, 0, m_vec, n_rows, finish)
        m_bw, n_bw = one_dir(NCHUNK - 1 - i, 1, m_vec, n_rows, finish)
        m_next = jnp.where(lane_row < HEADS, m_fw, m_bw)
        return (m_next,) + tuple(n_fw) + tuple(n_bw)

    init = tuple(jnp.zeros((1, LANES), F32) for _ in range(1 + 2 * (HEADS // 2)))
    return init, chunk_step


def _gla_parts(bqk_ref, bv_ref, br_ref, gcol_ref, w2_ref, gb_ref, nw_ref,
               y_ref, la_sc, osum_sc, st_sc):

    def pre_body(r, tot_min):
        rows = pl.ds(pl.multiple_of(r * ROW_BLK, ROW_BLK), ROW_BLK)
        lr = gcol_ref[rows, :LANES].astype(BF16)
        for d in range(2):
            z = _dot(lr, w2_ref[d]) + gb_ref[d]
            la = _log_sigmoid(z) * (1.0 / GLA_GATE_TEMP)
            la_sc[d, rows, :] = la
            for cc in range(ROW_BLK // CHUNK):
                chunk_total = jnp.sum(la[cc * CHUNK:(cc + 1) * CHUNK, :], axis=0, keepdims=True)
                tot_min = jnp.minimum(tot_min, chunk_total)
        return tot_min
    tot_min = lax.fori_loop(0, SEQ // ROW_BLK, pre_body, jnp.zeros((1, QK_W), F32))
    all_mild = jnp.min(tot_min) >= -GLA_MILD_LOG_DECAY

    st_sc[...] = jnp.zeros_like(st_sc)

    t_idx, s_idx = _tri_masks()
    eye = t_idx == s_idx
    lane2 = lax.broadcasted_iota(jnp.int32, (CHUNK, 2 * DV), 1)
    left_half = lane2 < DV
    bd_rows = lax.broadcasted_iota(jnp.int32, (2 * DV, LANES), 0) < DV
    bd_lanes = lax.broadcasted_iota(jnp.int32, (2 * DV, LANES), 1) < DK
    state_mask = bd_rows == bd_lanes

    def intra_mild(q2, k2, b_p, causal):
        qf = (q2 * jnp.exp(b_p)).astype(BF16)
        kf = (k2 * jnp.exp(-b_p)).astype(BF16)
        kb = jnp.concatenate([jnp.where(_half_mask(0), kf, jnp.zeros_like(kf)),
                              jnp.where(_half_mask(1), kf, jnp.zeros_like(kf))], axis=0)
        a_pair = _dot_nt(qf, kb)
        causal2 = jnp.concatenate([causal, causal], axis=1)
        return jnp.where(causal2, a_pair, 0.0)

    def intra_robust(q2, k2, la_p, seg_sum_p, direction, causal, strict_rev):
        k2b = k2.astype(BF16)
        acc = []
        for hh in range(2):
            qm = jnp.where(_half_mask(hh), q2, 0.0).astype(BF16)
            acc.append(jnp.where(eye, _dot_nt(qm, k2b), 0.0))
        for m in GLA_LEVELS + (1,):
            if m == 1:
                dq, dk = la_p, None
            else:
                same = _blk(t_idx, m) == _blk(s_idx, m)
                dq = seg_sum_p(same & causal)
                dk = seg_sum_p(same & strict_rev)
            qt = q2 * jnp.exp(dq)
            ktb = (k2 if dk is None else k2 * jnp.exp(dk)).astype(BF16)
            tb = _blk(t_idx, m)
            sb = _blk(s_idx, m)
            if direction == 0:
                lvl_mask = ((tb & 1) == 1) & (sb == tb - 1)
            else:
                lvl_mask = ((tb & 1) == 0) & (sb == tb + 1)
            for hh in range(2):
                qm = jnp.where(_half_mask(hh), qt, 0.0).astype(BF16)
                acc[hh] = acc[hh] + jnp.where(lvl_mask, _dot_nt(qm, ktb), 0.0)
        return jnp.concatenate(acc, axis=1)

    def one_dir(c, direction, mild, finish):
        r0 = pl.multiple_of(c * CHUNK, CHUNK)
        rows = pl.ds(r0, CHUNK)
        causal = (t_idx >= s_idx) if direction == 0 else (t_idx <= s_idx)
        strict_rev = (s_idx > t_idx) if direction == 0 else (s_idx < t_idx)
        t_last = CHUNK - 1 if direction == 0 else 0

        la = la_sc[direction, rows, :]
        b = _scan(la, 0, direction == 1, jnp.add, 0.0)
        b_tot = b[t_last:t_last + 1, :]
        eb = jnp.exp(b)
        k_decay = jnp.exp(b_tot - b)
        state_decay = jnp.exp(b_tot)

        for p in range(HEADS // 2):
            lanes = slice(p * LANES, (p + 1) * LANES)
            q2 = bqk_ref[rows, lanes].astype(F32) * (DK ** -0.5)
            k2 = bqk_ref[rows, QK_W + p * LANES:QK_W + (p + 1) * LANES].astype(F32)
            if mild:
                a_pair = intra_mild(q2, k2, b[:, lanes], causal)
            else:
                la_p3 = _split_bf16(la[:, lanes])

                def seg_sum_p(sel):
                    selb = sel.astype(BF16)
                    return _dot(selb, la_p3[0]) + _dot(selb, la_p3[1]) + _dot(selb, la_p3[2])
                a_pair = intra_robust(q2, k2, la[:, lanes], seg_sum_p, direction, causal, strict_rev)
            v_pair = bv_ref[rows, p * 2 * DV:(p + 1) * 2 * DV]
            zeros_v = jnp.zeros_like(v_pair)
            v_bd = jnp.concatenate([jnp.where(left_half, v_pair, zeros_v),
                                    jnp.where(left_half, zeros_v, v_pair)], axis=0)
            st = st_sc[direction, p]
            q_inter = (q2 * eb[:, lanes]).astype(BF16)
            o = _dot(a_pair.astype(BF16), v_bd) + _dot_nt(q_inter, st.astype(BF16))
            _accumulate_or_finish(o, osum_sc, rows, p, finish, br_ref, nw_ref, y_ref, lambda g: g * _sigmoid(g))
            k_state = (k2 * k_decay[:, lanes]).astype(BF16)
            upd_t = _dot_tn(v_pair, k_state)
            st_sc[direction, p] = st * state_decay[:, lanes] + jnp.where(state_mask, upd_t, 0.0)

    def chunk_step(i, mild, finish):
        one_dir(i, 0, mild, finish)
        one_dir(NCHUNK - 1 - i, 1, mild, finish)

    return all_mild, chunk_step


def _mixers_kernel(aqk_ref, av_ref, ao_ref, bqk_ref, bv_ref, br_ref, gcol_ref, grow_ref, gbrow_ref, gbcol_ref,
                   cw_ref, cb_ref, nwa_ref, w2_ref, gb_ref, nwb_ref, ya_ref, yb_ref,
                   q_sc, k_sc, hsum_sc, ct_sc, la_sc, osum_sc, st_sc):
    m_init, m_step = _mlstm_parts(aqk_ref, av_ref, ao_ref, gcol_ref, grow_ref, gbrow_ref, gbcol_ref,
                                  cw_ref, cb_ref, nwa_ref, ya_ref, q_sc, k_sc, hsum_sc, ct_sc)
    all_mild, g_step = _gla_parts(bqk_ref, bv_ref, br_ref, gcol_ref, w2_ref, gb_ref, nwb_ref,
                                  yb_ref, la_sc, osum_sc, st_sc)

    def run(mild):
        def half(finish):
            def chunk_body(i, carry):
                g_step(i, mild, finish)
                return m_step(i, carry, finish)
            return chunk_body
        carry = lax.fori_loop(0, NCHUNK // 2, half(False), m_init)
        lax.fori_loop(NCHUNK // 2, NCHUNK, half(True), carry)

    @pl.when(all_mild)
    def _():
        run(True)

    @pl.when(jnp.logical_not(all_mild))
    def _():
        run(False)


def _mixers(main, small, small_t, gb_row, gb_col, conv_w, conv_b, norm_a, w2, gb, norm_b):
    def blk(j):
        return pl.BlockSpec((SEQ, V_W), lambda b: (b, j))
    full = lambda shape: pl.BlockSpec(shape, lambda b: tuple(0 for _ in shape))
    return pl.pallas_call(
        _mixers_kernel,
        grid=(BATCH,),
        in_specs=[
            blk(0), blk(1), blk(2), blk(3), blk(4), blk(5),
            pl.BlockSpec((SEQ, SMALL_W), lambda b: (b, 0)),
            pl.BlockSpec((NCHUNK, N_GATES, CHUNK), lambda b: (b, 0, 0)),
            full((N_GATES, CHUNK)), full((1, SMALL_W)), full((3, 2 * QK_W)), full((1, 2 * QK_W)), full((1, V_W)),
            full((2, LANES, QK_W)), full((2, 1, QK_W)), full((1, V_W)),
        ],
        out_specs=[pl.BlockSpec((SEQ, V_W), lambda b: (b, 0)), pl.BlockSpec((SEQ, V_W), lambda b: (b, 0))],
        out_shape=[jax.ShapeDtypeStruct((TOKENS, V_W), BF16), jax.ShapeDtypeStruct((TOKENS, V_W), BF16)],
        scratch_shapes=[
            pltpu.VMEM((SEQ, QK_W), BF16),
            pltpu.VMEM((SEQ, QK_W), BF16),
            pltpu.VMEM((SEQ, V_W), F32),
            pltpu.VMEM((2, HEADS // 2, 2 * DV, 2 * DK), F32),
            pltpu.VMEM((2, SEQ, QK_W), F32),
            pltpu.VMEM((SEQ, V_W), F32),
            pltpu.VMEM((2, HEADS // 2, 2 * DV, 2 * DK), F32),
        ],
        compiler_params=pltpu.CompilerParams(
            dimension_semantics=("arbitrary",), vmem_limit_bytes=VMEM_LIMIT),
        name="mixers",
    )(main, main, main, main, main, main, small, small_t, gb_row, gb_col, conv_w, conv_b, norm_a, w2, gb, norm_b)


def _residual_and_next_norm(rows, x_ref, branch, g_post_ref, g_next_ref, o_ref, hn_ref):
    x_new = x_ref[rows, :] + branch * _rms_scale(branch) * g_post_ref[...]
    o_ref[rows, :] = x_new
    if hn_ref is not None:
        hn_ref[rows, :] = (x_new * _rms_scale(x_new) * g_next_ref[...]).astype(BF16)


def _row_blocks():
    return [slice(r, r + EPI_RB) for r in range(0, TM, EPI_RB)]


def _outproj_kernel(ya_ref, yb_ref, x_ref, w_ref, g_post_ref, g_next_ref, o_ref, hn_ref):
    for rows in _row_blocks():
        mix = _dot(ya_ref[rows, :], w_ref[:V_W, :]) + _dot(yb_ref[rows, :], w_ref[V_W:, :])
        _residual_and_next_norm(rows, x_ref, mix, g_post_ref, g_next_ref, o_ref, hn_ref)


def _row_tile(width):
    return pl.BlockSpec((TM, width), lambda i: (i, 0))


def _resident(shape):
    return pl.BlockSpec(shape, lambda i: (0, 0))


def _outproj(ya, yb, x2d, w, g_post, g_next):
    return pl.pallas_call(
        _outproj_kernel,
        grid=(TOKENS // TM,),
        in_specs=[_row_tile(V_W), _row_tile(V_W), _row_tile(D_MODEL), _resident((D_MODEL, D_MODEL)),
                  _resident((1, D_MODEL)), _resident((1, D_MODEL))],
        out_specs=[_row_tile(D_MODEL), _row_tile(D_MODEL)],
        out_shape=[jax.ShapeDtypeStruct((TOKENS, D_MODEL), F32),
                   jax.ShapeDtypeStruct((TOKENS, D_MODEL), BF16)],
        compiler_params=pltpu.CompilerParams(
            dimension_semantics=("arbitrary",), vmem_limit_bytes=VMEM_LIMIT),
        name="outproj",
    )(ya, yb, x2d, w, g_post, g_next)


def _gelu_tanh(x):
    k1 = -2.0 * 0.7978845608028654 * 1.4426950408889634
    return x / (1.0 + jnp.exp2(x * (k1 + (k1 * 0.044715) * (x * x))))


def _ffn_up_kernel(hn_ref, wg_ref, wu_ref, cw_ref, cb_ref, h_ref, g_sc, u_sc):
    g_sc[0:8, :] = jnp.zeros((8, FF_TN), F32)
    g_sc[SEQ + 8:SEQ + 16, :] = jnp.zeros((8, FF_TN), F32)
    n_blk = len(FF_BLOCKS)
    starts = [sum(FF_BLOCKS[:i]) for i in range(n_blk)]

    def project(i):
        rows = slice(starts[i], starts[i] + FF_BLOCKS[i])
        hn = hn_ref[rows, :]
        g_sc[8 + rows.start:8 + rows.stop, :] = _dot(hn, wg_ref[...])
        u_sc[rows, :] = _dot(hn, wu_ref[...])

    row = lax.broadcasted_iota(jnp.int32, (FF_ACT_RB, 1), 0)
    first_row = row == 0
    last_row = row == FF_ACT_RB - 1

    def activate(i):
        for j in range(FF_BLOCKS[i] // FF_ACT_RB):
            r0 = starts[i] + j * FF_ACT_RB
            cur = g_sc[r0 + 8:r0 + 8 + FF_ACT_RB, :]
            prev_row = g_sc[r0 + 7:r0 + 8, :]
            next_row = g_sc[r0 + 8 + FF_ACT_RB:r0 + 9 + FF_ACT_RB, :]
            prev = jnp.where(first_row, prev_row, pltpu.roll(cur, 1, 0))
            nxt = jnp.where(last_row, next_row, pltpu.roll(cur, FF_ACT_RB - 1, 0))
            gc = prev * cw_ref[0:1, :] + cur * cw_ref[1:2, :] + nxt * cw_ref[2:3, :] + cb_ref[...]
            h_ref[r0:r0 + FF_ACT_RB, :] = (_gelu_tanh(gc) * u_sc[r0:r0 + FF_ACT_RB, :]).astype(BF16)

    for i in range(n_blk + 1):
        if i < n_blk:
            project(i)
        if i >= 1:
            activate(i - 1)


def _ffn_up(hn, wg, wu, cw, cb):
    return pl.pallas_call(
        _ffn_up_kernel,
        grid=(BATCH, D_FF // FF_TN),
        in_specs=[
            pl.BlockSpec((SEQ, D_MODEL), lambda b, n: (b, 0)),
            pl.BlockSpec((D_MODEL, FF_TN), lambda b, n: (0, n)),
            pl.BlockSpec((D_MODEL, FF_TN), lambda b, n: (0, n)),
            pl.BlockSpec((3, FF_TN), lambda b, n: (0, n)),
            pl.BlockSpec((1, FF_TN), lambda b, n: (0, n)),
        ],
        out_specs=pl.BlockSpec((SEQ, FF_TN), lambda b, n: (b, n)),
        out_shape=jax.ShapeDtypeStruct((TOKENS, D_FF), BF16),
        scratch_shapes=[
            pltpu.VMEM((SEQ + 16, FF_TN), F32),
            pltpu.VMEM((SEQ, FF_TN), F32),
        ],
        compiler_params=pltpu.CompilerParams(
            dimension_semantics=("arbitrary", "arbitrary"), vmem_limit_bytes=VMEM_LIMIT),
        name="ffn_up",
    )(hn, wg, wu, cw, cb)


def _ffn_down_kernel(h_ref, x_ref, w_ref, g_post_ref, g_next_ref, o_ref, hn_ref):
    for rows in _row_blocks():
        ff = _dot(h_ref[rows, :], w_ref[...])
        _residual_and_next_norm(rows, x_ref, ff, g_post_ref, g_next_ref, o_ref, hn_ref)


def _ffn_down_last_kernel(h_ref, x_ref, w_ref, g_post_ref, o_ref):
    for rows in _row_blocks():
        ff = _dot(h_ref[rows, :], w_ref[...])
        _residual_and_next_norm(rows, x_ref, ff, g_post_ref, None, o_ref, None)


def _ffn_down(h, x2d, w, g_post, g_next):
    last = g_next is None
    in_specs = [_row_tile(D_FF), _row_tile(D_MODEL), _resident((D_FF, D_MODEL)), _resident((1, D_MODEL))]
    out_specs = [_row_tile(D_MODEL)]
    out_shape = [jax.ShapeDtypeStruct((TOKENS, D_MODEL), F32)]
    args = [h, x2d, w, g_post]
    if not last:
        in_specs.append(_resident((1, D_MODEL)))
        out_specs.append(_row_tile(D_MODEL))
        out_shape.append(jax.ShapeDtypeStruct((TOKENS, D_MODEL), BF16))
        args.append(g_next)
    outs = pl.pallas_call(
        _ffn_down_last_kernel if last else _ffn_down_kernel,
        grid=(TOKENS // TM,),
        in_specs=in_specs,
        out_specs=out_specs,
        out_shape=out_shape,
        compiler_params=pltpu.CompilerParams(
            dimension_semantics=("arbitrary",), vmem_limit_bytes=VMEM_LIMIT),
        name="ffn_down_last" if last else "ffn_down",
    )(*args)
    return (outs[0], None) if last else (outs[0], outs[1])


def _split_in_weights(w_in_l):
    offs = [0]
    for w in IN_WIDTHS:
        offs.append(offs[-1] + w)
    a_qk, a_v, a_o, gates, b_q, b_k, b_v, b_r, b_lr = (w_in_l[:, offs[i]:offs[i + 1]] for i in range(9))
    w_main = jnp.concatenate([a_qk, a_v, a_o, b_q, b_k, b_v, b_r], axis=1).astype(BF16)
    gates_i, gates_f = _gates_by_direction_head(gates, axis=1)
    pad = lambda n: jnp.zeros((D_MODEL, n), w_in_l.dtype)
    w_small = jnp.concatenate([gates_i, pad(LR_OFF - 2 * HEADS), b_lr, pad(LANES - LR_OFF - 2 * GLA_RANK),
                               gates_f, pad(SMALL_W - LANES - 2 * HEADS)], axis=1).astype(BF16)
    w_gates_t = jnp.concatenate([gates_i, gates_f], axis=1).T.astype(BF16)
    return w_main, w_small, w_gates_t


def _gates_by_direction_head(g, axis):
    i_fw, f_fw, i_bw, f_bw = jnp.split(g, 4, axis=axis)
    return jnp.concatenate([i_fw, i_bw], axis=axis), jnp.concatenate([f_fw, f_bw], axis=axis)


def kernel(x, norm_mix_pre, norm_mix_post, norm_ffn_pre, norm_ffn_post, w_in, mlstm_gate_b, mlstm_conv_w,
           mlstm_conv_b, mlstm_norm, gla_w2, gla_b, gla_norm, w_out, ffn_w_gate, ffn_w_up, ffn_conv_w,
           ffn_conv_b, ffn_w_down):
    def row(v):
        return v.reshape(1, -1).astype(F32)

    x2d = x.reshape(TOKENS, D_MODEL).astype(F32)
    hn = _rmsnorm(x2d, row(norm_mix_pre[0]))
    for l in range(DEPTH):
        w_main, w_small, w_small_t = _split_in_weights(w_in[l])
        main, small, small_t = _inproj(hn, w_main, w_small, w_small_t)

        bias_i, bias_f = _gates_by_direction_head(mlstm_gate_b[l].astype(F32), axis=0)
        gb_row = jnp.broadcast_to(jnp.concatenate([bias_i, bias_f])[:, None], (N_GATES, CHUNK))
        gb_col = jnp.concatenate([bias_i, jnp.zeros((LANES - 2 * HEADS,), F32),
                                  bias_f, jnp.zeros((SMALL_W - LANES - 2 * HEADS,), F32)]).reshape(1, SMALL_W)
        w2 = jnp.zeros((2, LANES, QK_W), F32)
        w2 = w2.at[0, LR_OFF:LR_OFF + GLA_RANK].set(gla_w2[l, 0])
        w2 = w2.at[1, LR_OFF + GLA_RANK:LR_OFF + 2 * GLA_RANK].set(gla_w2[l, 1])
        y_a, y_b = _mixers(main, small, small_t, gb_row, gb_col,
                           mlstm_conv_w[l].astype(F32), row(mlstm_conv_b[l]), row(mlstm_norm[l]),
                           w2.astype(BF16), gla_b[l].reshape(2, 1, QK_W).astype(F32), row(gla_norm[l]))

        x2d, hn = _outproj(y_a, y_b, x2d, w_out[l].astype(BF16), row(norm_mix_post[l]), row(norm_ffn_pre[l]))

        h = _ffn_up(hn, ffn_w_gate[l].astype(BF16), ffn_w_up[l].astype(BF16),
                    ffn_conv_w[l].astype(F32), row(ffn_conv_b[l]))
        g_next = row(norm_mix_pre[l + 1]) if l + 1 < DEPTH else None
        x2d, hn = _ffn_down(h, x2d, ffn_w_down[l].astype(BF16), row(norm_ffn_post[l]), g_next)
    return x2d.reshape(BATCH, SEQ, D_MODEL)
```

```python
import functools

import jax
import jax.numpy as jnp
from jax import lax
from jax.experimental import pallas as pl
from jax.experimental.pallas import tpu as pltpu

F32 = jnp.float32
BF16 = jnp.bfloat16

D_MODEL = 1024
BATCH = 8
SEQ = 2048
DEPTH = 4
TOKENS = BATCH * SEQ
HEADS = 4
DK = 64
DV = 128
QK_W = HEADS * DK
V_W = HEADS * DV
GLA_RANK = 16
GLA_GATE_TEMP = 16.0
D_FF = 2816
EPS = 1e-6
IN_WIDTHS = (2 * QK_W, V_W, V_W, 4 * HEADS, QK_W, QK_W, V_W, V_W, 2 * GLA_RANK)
N_MAIN = 2 * QK_W + V_W + V_W + QK_W + QK_W + V_W + V_W
N_GATES = 4 * HEADS
SMALL_W = 256
LR_OFF = 16

LANES = 128
CHUNK = 128
NCHUNK = SEQ // CHUNK
ROW_BLK = 256
TM = 1024
EPI_RB = 256
FF_TN = 256
FF_BLOCKS = (512, 512, 512, 512)
assert sum(FF_BLOCKS) == SEQ
FF_ACT_RB = 128
GLA_LEVELS = (64, 32, 16, 8, 4, 2)
GLA_MILD_LOG_DECAY = 60.0
VMEM_LIMIT = 56 * 1024 * 1024


def _dot(a, b):
    return jnp.dot(a, b, preferred_element_type=F32)


def _dot_nt(a, b):
    return lax.dot_general(a, b, (((1,), (1,)), ((), ())), preferred_element_type=F32)


def _dot_tn(a, b):
    return lax.dot_general(a, b, (((0,), (0,)), ((), ())), preferred_element_type=F32)


def _scan(x, axis, reverse, op, identity):
    n = x.shape[axis]
    idx = lax.broadcasted_iota(jnp.int32, x.shape, axis)
    k = 1
    while k < n:
        if reverse:
            shifted = jnp.where(idx < n - k, pltpu.roll(x, n - k, axis), identity)
        else:
            shifted = jnp.where(idx >= k, pltpu.roll(x, k, axis), identity)
        x = op(x, shifted)
        k *= 2
    return x


def _split_bf16(x):
    hi = x.astype(BF16)
    r1 = x - hi.astype(F32)
    mid = r1.astype(BF16)
    lo = (r1 - mid.astype(F32)).astype(BF16)
    return hi, mid, lo


def _sel_dot(sel, x):
    hi, mid, lo = _split_bf16(x)
    return _dot(sel, hi) + _dot(sel, mid) + _dot(sel, lo)


def _dot_sel(x, sel):
    hi, mid, lo = _split_bf16(x)
    return _dot(hi, sel) + _dot(mid, sel) + _dot(lo, sel)


def _log_sigmoid(x):
    return jnp.minimum(x, 0.0) - jnp.log(1.0 + jnp.exp(-jnp.abs(x)))


def _sigmoid(x):
    return 1.0 / (1.0 + jnp.exp(-x))


def _rms_scale(x):
    return lax.rsqrt(jnp.mean(x * x, axis=-1, keepdims=True) + EPS)


def _project_in(hn, wm_ref, ws_ref, wst_ref, main_ref, small_ref, smallt_ref):
    for j in range(N_MAIN // 512):
        cols = slice(j * 512, (j + 1) * 512)
        main_ref[:, cols] = _dot(hn, wm_ref[:, cols]).astype(BF16)
    small_ref[...] = _dot(hn, ws_ref[...])
    st = _dot_nt(wst_ref[...], hn)
    for j in range(TM // CHUNK):
        smallt_ref[j] = st[:, j * CHUNK:(j + 1) * CHUNK]


def _inproj_kernel(hn_ref, wm_ref, ws_ref, wst_ref, main_ref, small_ref, smallt_ref):
    _project_in(hn_ref[...], wm_ref, ws_ref, wst_ref, main_ref, small_ref, smallt_ref)


def _norm_inproj_kernel(x_ref, g_ref, wm_ref, ws_ref, wst_ref, main_ref, small_ref, smallt_ref):
    x = x_ref[...]
    hn = (x * _rms_scale(x) * g_ref[...]).astype(BF16)
    _project_in(hn, wm_ref, ws_ref, wst_ref, main_ref, small_ref, smallt_ref)


def _inproj(act, w_main, w_small, w_small_t, norm_g=None):
    lead_specs = [pl.BlockSpec((TM, D_MODEL), lambda i: (i, 0))]
    lead_args = [act]
    if norm_g is not None:
        lead_specs.append(pl.BlockSpec((1, D_MODEL), lambda i: (0, 0)))
        lead_args.append(norm_g)
    return pl.pallas_call(
        _inproj_kernel if norm_g is None else _norm_inproj_kernel,
        grid=(TOKENS // TM,),
        in_specs=lead_specs + [
            pl.BlockSpec((D_MODEL, N_MAIN), lambda i: (0, 0)),
            pl.BlockSpec((D_MODEL, SMALL_W), lambda i: (0, 0)),
            pl.BlockSpec((N_GATES, D_MODEL), lambda i: (0, 0)),
        ],
        out_specs=[
            pl.BlockSpec((TM, N_MAIN), lambda i: (i, 0)),
            pl.BlockSpec((TM, SMALL_W), lambda i: (i, 0)),
            pl.BlockSpec((TM // CHUNK, N_GATES, CHUNK), lambda i: (i, 0, 0)),
        ],
        out_shape=[
            jax.ShapeDtypeStruct((TOKENS, N_MAIN), BF16),
            jax.ShapeDtypeStruct((TOKENS, SMALL_W), F32),
            jax.ShapeDtypeStruct((TOKENS // CHUNK, N_GATES, CHUNK), F32),
        ],
        compiler_params=pltpu.CompilerParams(
            dimension_semantics=("arbitrary",), vmem_limit_bytes=VMEM_LIMIT),
        name="inproj" if norm_g is None else "norm_inproj",
    )(*lead_args, w_main, w_small, w_small_t)


def _tri_masks():
    t = lax.broadcasted_iota(jnp.int32, (CHUNK, CHUNK), 0)
    s = lax.broadcasted_iota(jnp.int32, (CHUNK, CHUNK), 1)
    return t, s


def _blk(idx, m):
    return jnp.right_shift(idx, m.bit_length() - 1)


def _half_mask(hh):
    lane = lax.broadcasted_iota(jnp.int32, (CHUNK, LANES), 1)
    return (lane < DK) if hh == 0 else (lane >= DK)


def _accumulate_or_finish(part, sum_sc, rows, pair, finish, gate_ref, nw_ref, y_ref, gate_fn):
    cols2 = slice(pair * 2 * DV, (pair + 1) * 2 * DV)
    if not finish:
        sum_sc[rows, cols2] = part
        return
    total = sum_sc[rows, cols2] + part
    for hh in range(2):
        cols = slice(pair * 2 * DV + hh * DV, pair * 2 * DV + (hh + 1) * DV)
        hs = total[:, hh * DV:(hh + 1) * DV]
        yn = hs * _rms_scale(hs) * nw_ref[:, cols]
        gate = gate_ref[rows, cols].astype(F32)
        y_ref[rows, cols] = (gate_fn(gate) * yn).astype(BF16)


def _mlstm_parts(aqk_ref, av_ref, ao_ref, gcol_ref, grow_ref, gbrow_ref, gbcol_ref, cw_ref, cb_ref, nw_ref,
                 y_ref, q_sc, k_sc, hsum_sc, ct_sc):

    def conv_chunk(c):
        r0 = pl.multiple_of(c * CHUNK, CHUNK)
        cur = aqk_ref[pl.ds(r0, CHUNK), :].astype(F32)
        p0 = pl.multiple_of(jnp.maximum(r0 - 16, 0), 16)
        n0 = pl.multiple_of(jnp.minimum(r0 + CHUNK, SEQ - 16), 16)
        prev_row = aqk_ref[pl.ds(p0, 16), :].astype(F32)[15:16]
        next_row = aqk_ref[pl.ds(n0, 16), :].astype(F32)[0:1]
        prev_row = jnp.where(c == 0, 0.0, prev_row)
        next_row = jnp.where(c == NCHUNK - 1, 0.0, next_row)
        row = lax.broadcasted_iota(jnp.int32, (CHUNK, 1), 0)
        prev = jnp.where(row == 0, prev_row, pltpu.roll(cur, 1, 0))
        nxt = jnp.where(row == CHUNK - 1, next_row, pltpu.roll(cur, CHUNK - 1, 0))
        z = prev * cw_ref[0:1, :] + cur * cw_ref[1:2, :] + nxt * cw_ref[2:3, :] + cb_ref[...]
        act = z * _sigmoid(z)
        return act[:, :QK_W].astype(BF16), (act[:, QK_W:] * (DK ** -0.5)).astype(BF16)

    ct_sc[...] = jnp.zeros_like(ct_sc)

    t_idx, s_idx = _tri_masks()
    le = t_idx >= s_idx
    ge = t_idx <= s_idx
    tri_le = le.astype(BF16)
    tri_ge = ge.astype(BF16)
    lane_row = lax.broadcasted_iota(jnp.int32, (1, LANES), 1)
    half0 = _half_mask(0)
    half0_row = lane_row < DK
    lane2 = lax.broadcasted_iota(jnp.int32, (CHUNK, 2 * DV), 1)
    left_half = lane2 < DV
    bd_r = lax.broadcasted_iota(jnp.int32, (2 * DV, 2 * DV), 0) < DV
    bd_l = lax.broadcasted_iota(jnp.int32, (2 * DV, 2 * DV), 1) < DV
    ones_bd = (bd_r == bd_l).astype(BF16)
    st_r = lax.broadcasted_iota(jnp.int32, (2 * DV, LANES), 0) < DV
    st_l = lax.broadcasted_iota(jnp.int32, (2 * DV, LANES), 1) < DK
    state_mask = st_r == st_l

    def lane_bc(x, j):
        return jnp.broadcast_to(x[:, j:j + 1], (x.shape[0], LANES))

    def one_dir(c, direction, m_vec, n_rows, finish):
        r0 = pl.multiple_of(c * CHUNK, CHUNK)
        rows = pl.ds(r0, CHUNK)
        mask = le if direction == 0 else ge
        t_last = CHUNK - 1 if direction == 0 else 0

        g_row = grow_ref[c] + gbrow_ref[...]
        lf_row = _log_sigmoid(g_row[2 * HEADS:, :])
        b_row = _dot_sel(lf_row, tri_ge if direction == 0 else tri_le)
        a_row = g_row[:2 * HEADS, :] - b_row
        g_col = gcol_ref[rows, :] + gbcol_ref[...]
        lf_col = _log_sigmoid(g_col[:, LANES:])
        b_col = _sel_dot(tri_le if direction == 0 else tri_ge, lf_col)
        a_col = g_col[:, :LANES] - b_col
        pmax = _scan(a_col, 0, direction == 1, jnp.maximum, -jnp.inf)
        u = jnp.maximum(m_vec, pmax)
        u_last = u[t_last:t_last + 1, :]
        m_new = b_col[t_last:t_last + 1, :] + u_last
        thr = jnp.exp(-b_col - u)
        w = jnp.exp(a_col - u_last)
        decay = jnp.exp(m_vec - u_last)

        if not finish:
            q_all, k_all = conv_chunk(c)
            q_sc[rows, :] = q_all
            k_sc[rows, :] = k_all

        n_new = []
        for p in range(HEADS // 2):
            lanes = slice(p * LANES, (p + 1) * LANES)
            hd = [direction * HEADS + 2 * p, direction * HEADS + 2 * p + 1]
            if finish:
                q2 = q_sc[rows, lanes]
                k2 = k_sc[rows, lanes]
            else:
                q2 = q_all[:, lanes]
                k2 = k_all[:, lanes]
            u_bc = [lane_bc(u, j) for j in hd]
            e_pair = jnp.concatenate(
                [jnp.where(mask, jnp.exp(jnp.broadcast_to(a_row[j:j + 1, :], (CHUNK, CHUNK)) - ub), 0.0)
                 for j, ub in zip(hd, u_bc)], axis=1)
            zeros_k = jnp.zeros_like(k2)
            kb = jnp.concatenate([jnp.where(half0, k2, zeros_k), jnp.where(half0, zeros_k, k2)], axis=0)
            pm = (_dot_nt(q2, kb) * e_pair).astype(BF16)
            v_pair = av_ref[rows, p * 2 * DV:(p + 1) * 2 * DV]
            zeros_v = jnp.zeros_like(v_pair)
            v_bd = jnp.concatenate([jnp.where(left_half, v_pair, zeros_v),
                                    jnp.where(left_half, zeros_v, v_pair)], axis=0)
            inter = jnp.where(half0, jnp.exp(lane_bc(m_vec, hd[0]) - u_bc[0]),
                              jnp.exp(lane_bc(m_vec, hd[1]) - u_bc[1]))
            qi = (q2.astype(F32) * inter).astype(BF16)
            ct = ct_sc[direction, p]
            n_row = n_rows[direction * (HEADS // 2) + p]
            n_bd = jnp.where(state_mask, jnp.broadcast_to(n_row, (2 * DV, LANES)), 0.0).astype(BF16)
            num = _dot(pm, v_bd) + _dot_nt(qi, ct.astype(BF16))
            den = _dot(pm, ones_bd) + _dot_nt(qi, n_bd)
            thr_pair = jnp.concatenate([lane_bc(thr, hd[0]), lane_bc(thr, hd[1])], axis=1)
            _accumulate_or_finish(num / jnp.maximum(jnp.abs(den), thr_pair), hsum_sc, rows, p, finish,
                                  ao_ref, nw_ref, y_ref, _sigmoid)
            w_pair = jnp.where(half0, lane_bc(w, hd[0]), lane_bc(w, hd[1]))
            wk = k2.astype(F32) * w_pair
            decay_pair = jnp.where(half0_row, lane_bc(decay, hd[0]), lane_bc(decay, hd[1]))
            upd_t = _dot_tn(v_pair, wk.astype(BF16))
            ct_sc[direction, p] = ct * decay_pair + jnp.where(state_mask, upd_t, 0.0)
            n_new.append(n_row * decay_pair + jnp.sum(wk, axis=0, keepdims=True))
        return m_new, n_new

    def chunk_step(i, carry, finish):
        m_vec, n_rows = carry[0], carry[1:]
        m_fw, n_fw = one_dir(i, 0, m_vec, n_rows, finish)
        m_bw, n_bw = one_dir(NCHUNK - 1 - i, 1, m_vec, n_rows, finish)
        m_next = jnp.where(lane_row < HEADS, m_fw, m_bw)
        return (m_next,) + tuple(n_fw) + tuple(n_bw)

    init = tuple(jnp.zeros((1, LANES), F32) for _ in range(1 + 2 * (HEADS // 2)))
    return init, chunk_step


def _gla_parts(bqk_ref, bv_ref, br_ref, gcol_ref, w2_ref, gb_ref, nw_ref,
               y_ref, la_sc, osum_sc, st_sc):

    def pre_body(r, tot_min):
        rows = pl.ds(pl.multiple_of(r * ROW_BLK, ROW_BLK), ROW_BLK)
        lr = gcol_ref[rows, :LANES].astype(BF16)
        for d in range(2):
            z = _dot(lr, w2_ref[d]) + gb_ref[d]
            la = _log_sigmoid(z) * (1.0 / GLA_GATE_TEMP)
            la_sc[d, rows, :] = la
            for cc in range(ROW_BLK // CHUNK):
                chunk_total = jnp.sum(la[cc * CHUNK:(cc + 1) * CHUNK, :], axis=0, keepdims=True)
                tot_min = jnp.minimum(tot_min, chunk_total)
        return tot_min
    tot_min = lax.fori_loop(0, SEQ // ROW_BLK, pre_body, jnp.zeros((1, QK_W), F32))
    all_mild = jnp.min(tot_min) >= -GLA_MILD_LOG_DECAY

    st_sc[...] = jnp.zeros_like(st_sc)

    t_idx, s_idx = _tri_masks()
    eye = t_idx == s_idx
    lane2 = lax.broadcasted_iota(jnp.int32, (CHUNK, 2 * DV), 1)
    left_half = lane2 < DV
    bd_rows = lax.broadcasted_iota(jnp.int32, (2 * DV, LANES), 0) < DV
    bd_lanes = lax.broadcasted_iota(jnp.int32, (2 * DV, LANES), 1) < DK
    state_mask = bd_rows == bd_lanes

    def intra_mild(q2, k2, b_p, causal):
        qf = (q2 * jnp.exp(b_p)).astype(BF16)
        kf = (k2 * jnp.exp(-b_p)).astype(BF16)
        kb = jnp.concatenate([jnp.where(_half_mask(0), kf, jnp.zeros_like(kf)),
                              jnp.where(_half_mask(1), kf, jnp.zeros_like(kf))], axis=0)
        a_pair = _dot_nt(qf, kb)
        causal2 = jnp.concatenate([causal, causal], axis=1)
        return jnp.where(causal2, a_pair, 0.0)

    def intra_robust(q2, k2, la_p, seg_sum_p, direction, causal, strict_rev):
        k2b = k2.astype(BF16)
        acc = []
        for hh in range(2):
            qm = jnp.where(_half_mask(hh), q2, 0.0).astype(BF16)
            acc.append(jnp.where(eye, _dot_nt(qm, k2b), 0.0))
        for m in GLA_LEVELS + (1,):
            if m == 1:
                dq, dk = la_p, None
            else:
                same = _blk(t_idx, m) == _blk(s_idx, m)
                dq = seg_sum_p(same & causal)
                dk = seg_sum_p(same & strict_rev)
            qt = q2 * jnp.exp(dq)
            ktb = (k2 if dk is None else k2 * jnp.exp(dk)).astype(BF16)
            tb = _blk(t_idx, m)
            sb = _blk(s_idx, m)
            if direction == 0:
                lvl_mask = ((tb & 1) == 1) & (sb == tb - 1)
            else:
                lvl_mask = ((tb & 1) == 0) & (sb == tb + 1)
            for hh in range(2):
                qm = jnp.where(_half_mask(hh), qt, 0.0).astype(BF16)
                acc[hh] = acc[hh] + jnp.where(lvl_mask, _dot_nt(qm, ktb), 0.0)
        return jnp.concatenate(acc, axis=1)

    def one_dir(c, direction, mild, finish):
        r0 = pl.multiple_of(c * CHUNK, CHUNK)
        rows = pl.ds(r0, CHUNK)
        causal = (t_idx >= s_idx) if direction == 0 else (t_idx <= s_idx)
        strict_rev = (s_idx > t_idx) if direction == 0 else (s_idx < t_idx)
        t_last = CHUNK - 1 if direction == 0 else 0

        la = la_sc[direction, rows, :]
        b = _scan(la, 0, direction == 1, jnp.add, 0.0)
        b_tot = b[t_last:t_last + 1, :]
        eb = jnp.exp(b)
        k_decay = jnp.exp(b_tot - b)
        state_decay = jnp.exp(b_tot)

        for p in range(HEADS // 2):
            lanes = slice(p * LANES, (p + 1) * LANES)
            q2 = bqk_ref[rows, lanes].astype(F32) * (DK ** -0.5)
            k2 = bqk_ref[rows, QK_W + p * LANES:QK_W + (p + 1) * LANES].astype(F32)
            if mild:
                a_pair = intra_mild(q2, k2, b[:, lanes], causal)
            else:
                la_p3 = _split_bf16(la[:, lanes])

                def seg_sum_p(sel):
                    selb = sel.astype(BF16)
                    return _dot(selb, la_p3[0]) + _dot(selb, la_p3[1]) + _dot(selb, la_p3[2])
                a_pair = intra_robust(q2, k2, la[:, lanes], seg_sum_p, direction, causal, strict_rev)
            v_pair = bv_ref[rows, p * 2 * DV:(p + 1) * 2 * DV]
            zeros_v = jnp.zeros_like(v_pair)
            v_bd = jnp.concatenate([jnp.where(left_half, v_pair, zeros_v),
                                    jnp.where(left_half, zeros_v, v_pair)], axis=0)
            st = st_sc[direction, p]
            q_inter = (q2 * eb[:, lanes]).astype(BF16)
            o = _dot(a_pair.astype(BF16), v_bd) + _dot_nt(q_inter, st.astype(BF16))
            _accumulate_or_finish(o, osum_sc, rows, p, finish, br_ref, nw_ref, y_ref, lambda g: g * _sigmoid(g))
            k_state = (k2 * k_decay[:, lanes]).astype(BF16)
            upd_t = _dot_tn(v_pair, k_state)
            st_sc[direction, p] = st * state_decay[:, lanes] + jnp.where(state_mask, upd_t, 0.0)

    def chunk_step(i, mild, finish):
        one_dir(i, 0, mild, finish)
        one_dir(NCHUNK - 1 - i, 1, mild, finish)

    return all_mild, chunk_step


def _mixers_kernel(aqk_ref, av_ref, ao_ref, bqk_ref, bv_ref, br_ref, gcol_ref, grow_ref, gbrow_ref, gbcol_ref,
                   cw_ref, cb_ref, nwa_ref, w2_ref, gb_ref, nwb_ref, ya_ref, yb_ref,
                   q_sc, k_sc, hsum_sc, ct_sc, la_sc, osum_sc, st_sc):
    m_init, m_step = _mlstm_parts(aqk_ref, av_ref, ao_ref, gcol_ref, grow_ref, gbrow_ref, gbcol_ref,
                                  cw_ref, cb_ref, nwa_ref, ya_ref, q_sc, k_sc, hsum_sc, ct_sc)
    all_mild, g_step = _gla_parts(bqk_ref, bv_ref, br_ref, gcol_ref, w2_ref, gb_ref, nwb_ref,
                                  yb_ref, la_sc, osum_sc, st_sc)

    def run(mild):
        def half(finish):
            def chunk_body(i, carry):
                g_step(i, mild, finish)
                return m_step(i, carry, finish)
            return chunk_body
        carry = lax.fori_loop(0, NCHUNK // 2, half(False), m_init)
        lax.fori_loop(NCHUNK // 2, NCHUNK, half(True), carry)

    @pl.when(all_mild)
    def _():
        run(True)

    @pl.when(jnp.logical_not(all_mild))
    def _():
        run(False)


def _mixers(main, small, small_t, gb_row, gb_col, conv_w, conv_b, norm_a, w2, gb, norm_b):
    def blk(j):
        return pl.BlockSpec((SEQ, V_W), lambda b: (b, j))
    full = lambda shape: pl.BlockSpec(shape, lambda b: tuple(0 for _ in shape))
    return pl.pallas_call(
        _mixers_kernel,
        grid=(BATCH,),
        in_specs=[
            blk(0), blk(1), blk(2), blk(3), blk(4), blk(5),
            pl.BlockSpec((SEQ, SMALL_W), lambda b: (b, 0)),
            pl.BlockSpec((NCHUNK, N_GATES, CHUNK), lambda b: (b, 0, 0)),
            full((N_GATES, CHUNK)), full((1, SMALL_W)), full((3, 2 * QK_W)), full((1, 2 * QK_W)), full((1, V_W)),
            full((2, LANES, QK_W)), full((2, 1, QK_W)), full((1, V_W)),
        ],
        out_specs=[pl.BlockSpec((SEQ, V_W), lambda b: (b, 0)), pl.BlockSpec((SEQ, V_W), lambda b: (b, 0))],
        out_shape=[jax.ShapeDtypeStruct((TOKENS, V_W), BF16), jax.ShapeDtypeStruct((TOKENS, V_W), BF16)],
        scratch_shapes=[
            pltpu.VMEM((SEQ, QK_W), BF16),
            pltpu.VMEM((SEQ, QK_W), BF16),
            pltpu.VMEM((SEQ, V_W), F32),
            pltpu.VMEM((2, HEADS // 2, 2 * DV, 2 * DK), F32),
            pltpu.VMEM((2, SEQ, QK_W), F32),
            pltpu.VMEM((SEQ, V_W), F32),
            pltpu.VMEM((2, HEADS // 2, 2 * DV, 2 * DK), F32),
        ],
        compiler_params=pltpu.CompilerParams(
            dimension_semantics=("arbitrary",), vmem_limit_bytes=VMEM_LIMIT),
        name="mixers",
    )(main, main, main, main, main, main, small, small_t, gb_row, gb_col, conv_w, conv_b, norm_a, w2, gb, norm_b)


def _residual_and_next_norm(rows, x_ref, branch, g_post_ref, g_next_ref, o_ref, hn_ref):
    x_new = x_ref[rows, :] + branch * _rms_scale(branch) * g_post_ref[...]
    o_ref[rows, :] = x_new
    if hn_ref is not None:
        hn_ref[rows, :] = (x_new * _rms_scale(x_new) * g_next_ref[...]).astype(BF16)


def _row_blocks():
    return [slice(r, r + EPI_RB) for r in range(0, TM, EPI_RB)]


def _outproj_kernel(ya_ref, yb_ref, x_ref, w_ref, g_post_ref, g_next_ref, o_ref, hn_ref):
    for rows in _row_blocks():
        mix = _dot(ya_ref[rows, :], w_ref[:V_W, :]) + _dot(yb_ref[rows, :], w_ref[V_W:, :])
        _residual_and_next_norm(rows, x_ref, mix, g_post_ref, g_next_ref, o_ref, hn_ref)


def _row_tile(width):
    return pl.BlockSpec((TM, width), lambda i: (i, 0))


def _resident(shape):
    return pl.BlockSpec(shape, lambda i: (0, 0))


def _outproj(ya, yb, x2d, w, g_post, g_next):
    return pl.pallas_call(
        _outproj_kernel,
        grid=(TOKENS // TM,),
        in_specs=[_row_tile(V_W), _row_tile(V_W), _row_tile(D_MODEL), _resident((D_MODEL, D_MODEL)),
                  _resident((1, D_MODEL)), _resident((1, D_MODEL))],
        out_specs=[_row_tile(D_MODEL), _row_tile(D_MODEL)],
        out_shape=[jax.ShapeDtypeStruct((TOKENS, D_MODEL), F32),
                   jax.ShapeDtypeStruct((TOKENS, D_MODEL), BF16)],
        compiler_params=pltpu.CompilerParams(
            dimension_semantics=("arbitrary",), vmem_limit_bytes=VMEM_LIMIT),
        name="outproj",
    )(ya, yb, x2d, w, g_post, g_next)


def _gelu_tanh(x):
    k1 = -2.0 * 0.7978845608028654 * 1.4426950408889634
    return x / (1.0 + jnp.exp2(x * (k1 + (k1 * 0.044715) * (x * x))))


def _ffn_up_kernel(hn_ref, wg_ref, wu_ref, cw_ref, cb_ref, h_ref, g_sc, u_sc):
    g_sc[0:8, :] = jnp.zeros((8, FF_TN), F32)
    g_sc[SEQ + 8:SEQ + 16, :] = jnp.zeros((8, FF_TN), F32)
    n_blk = len(FF_BLOCKS)
    starts = [sum(FF_BLOCKS[:i]) for i in range(n_blk)]

    def project(i):
        rows = slice(starts[i], starts[i] + FF_BLOCKS[i])
        hn = hn_ref[rows, :]
        g_sc[8 + rows.start:8 + rows.stop, :] = _dot(hn, wg_ref[...])
        u_sc[rows, :] = _dot(hn, wu_ref[...])

    row = lax.broadcasted_iota(jnp.int32, (FF_ACT_RB, 1), 0)
    first_row = row == 0
    last_row = row == FF_ACT_RB - 1

    def activate(i):
        for j in range(FF_BLOCKS[i] // FF_ACT_RB):
            r0 = starts[i] + j * FF_ACT_RB
            cur = g_sc[r0 + 8:r0 + 8 + FF_ACT_RB, :]
            prev_row = g_sc[r0 + 7:r0 + 8, :]
            next_row = g_sc[r0 + 8 + FF_ACT_RB:r0 + 9 + FF_ACT_RB, :]
            prev = jnp.where(first_row, prev_row, pltpu.roll(cur, 1, 0))
            nxt = jnp.where(last_row, next_row, pltpu.roll(cur, FF_ACT_RB - 1, 0))
            gc = prev * cw_ref[0:1, :] + cur * cw_ref[1:2, :] + nxt * cw_ref[2:3, :] + cb_ref[...]
            h_ref[r0:r0 + FF_ACT_RB, :] = (_gelu_tanh(gc) * u_sc[r0:r0 + FF_ACT_RB, :]).astype(BF16)

    for i in range(n_blk + 1):
        if i < n_blk:
            project(i)
        if i >= 1:
            activate(i - 1)


def _ffn_up(hn, wg, wu, cw, cb):
    return pl.pallas_call(
        _ffn_up_kernel,
        grid=(BATCH, D_FF // FF_TN),
        in_specs=[
            pl.BlockSpec((SEQ, D_MODEL), lambda b, n: (b, 0)),
            pl.BlockSpec((D_MODEL, FF_TN), lambda b, n: (0, n)),
            pl.BlockSpec((D_MODEL, FF_TN), lambda b, n: (0, n)),
            pl.BlockSpec((3, FF_TN), lambda b, n: (0, n)),
            pl.BlockSpec((1, FF_TN), lambda b, n: (0, n)),
        ],
        out_specs=pl.BlockSpec((SEQ, FF_TN), lambda b, n: (b, n)),
        out_shape=jax.ShapeDtypeStruct((TOKENS, D_FF), BF16),
        scratch_shapes=[
            pltpu.VMEM((SEQ + 16, FF_TN), F32),
            pltpu.VMEM((SEQ, FF_TN), F32),
        ],
        compiler_params=pltpu.CompilerParams(
            dimension_semantics=("arbitrary", "arbitrary"), vmem_limit_bytes=VMEM_LIMIT),
        name="ffn_up",
    )(hn, wg, wu, cw, cb)


def _ffn_down_kernel(h_ref, x_ref, w_ref, g_post_ref, g_next_ref, o_ref, hn_ref):
    for rows in _row_blocks():
        ff = _dot(h_ref[rows, :], w_ref[...])
        _residual_and_next_norm(rows, x_ref, ff, g_post_ref, g_next_ref, o_ref, hn_ref)


def _ffn_down_last_kernel(h_ref, x_ref, w_ref, g_post_ref, o_ref):
    for rows in _row_blocks():
        ff = _dot(h_ref[rows, :], w_ref[...])
        _residual_and_next_norm(rows, x_ref, ff, g_post_ref, None, o_ref, None)


def _ffn_down(h, x2d, w, g_post, g_next):
    last = g_next is None
    in_specs = [_row_tile(D_FF), _row_tile(D_MODEL), _resident((D_FF, D_MODEL)), _resident((1, D_MODEL))]
    out_specs = [_row_tile(D_MODEL)]
    out_shape = [jax.ShapeDtypeStruct((TOKENS, D_MODEL), F32)]
    args = [h, x2d, w, g_post]
    if not last:
        in_specs.append(_resident((1, D_MODEL)))
        out_specs.append(_row_tile(D_MODEL))
        out_shape.append(jax.ShapeDtypeStruct((TOKENS, D_MODEL), BF16))
        args.append(g_next)
    outs = pl.pallas_call(
        _ffn_down_last_kernel if last else _ffn_down_kernel,
        grid=(TOKENS // TM,),
        in_specs=in_specs,
        out_specs=out_specs,
        out_shape=out_shape,
        compiler_params=pltpu.CompilerParams(
            dimension_semantics=("arbitrary",), vmem_limit_bytes=VMEM_LIMIT),
        name="ffn_down_last" if last else "ffn_down",
    )(*args)
    return (outs[0], None) if last else (outs[0], outs[1])


def _split_in_weights(w_in_l):
    offs = [0]
    for w in IN_WIDTHS:
        offs.append(offs[-1] + w)
    a_qk, a_v, a_o, gates, b_q, b_k, b_v, b_r, b_lr = (w_in_l[:, offs[i]:offs[i + 1]] for i in range(9))
    w_main = jnp.concatenate([a_qk, a_v, a_o, b_q, b_k, b_v, b_r], axis=1).astype(BF16)
    gates_i, gates_f = _gates_by_direction_head(gates, axis=1)
    pad = lambda n: jnp.zeros((D_MODEL, n), w_in_l.dtype)
    w_small = jnp.concatenate([gates_i, pad(LR_OFF - 2 * HEADS), b_lr, pad(LANES - LR_OFF - 2 * GLA_RANK),
                               gates_f, pad(SMALL_W - LANES - 2 * HEADS)], axis=1).astype(BF16)
    w_gates_t = jnp.concatenate([gates_i, gates_f], axis=1).T.astype(BF16)
    return w_main, w_small, w_gates_t


def _gates_by_direction_head(g, axis):
    i_fw, f_fw, i_bw, f_bw = jnp.split(g, 4, axis=axis)
    return jnp.concatenate([i_fw, i_bw], axis=axis), jnp.concatenate([f_fw, f_bw], axis=axis)


def kernel(x, norm_mix_pre, norm_mix_post, norm_ffn_pre, norm_ffn_post, w_in, mlstm_gate_b, mlstm_conv_w,
           mlstm_conv_b, mlstm_norm, gla_w2, gla_b, gla_norm, w_out, ffn_w_gate, ffn_w_up, ffn_conv_w,
           ffn_conv_b, ffn_w_down):
    def row(v):
        return v.reshape(1, -1).astype(F32)

    x2d = x.reshape(TOKENS, D_MODEL).astype(F32)
    hn = None
    for l in range(DEPTH):
        w_main, w_small, w_small_t = _split_in_weights(w_in[l])
        if l == 0:
            main, small, small_t = _inproj(x2d, w_main, w_small, w_small_t, norm_g=row(norm_mix_pre[0]))
        else:
            main, small, small_t = _inproj(hn, w_main, w_small, w_small_t)

        bias_i, bias_f = _gates_by_direction_head(mlstm_gate_b[l].astype(F32), axis=0)
        gb_row = jnp.broadcast_to(jnp.concatenate([bias_i, bias_f])[:, None], (N_GATES, CHUNK))
        gb_col = jnp.concatenate([bias_i, jnp.zeros((LANES - 2 * HEADS,), F32),
                                  bias_f, jnp.zeros((SMALL_W - LANES - 2 * HEADS,), F32)]).reshape(1, SMALL_W)
        w2 = jnp.zeros((2, LANES, QK_W), F32)
        w2 = w2.at[0, LR_OFF:LR_OFF + GLA_RANK].set(gla_w2[l, 0])
        w2 = w2.at[1, LR_OFF + GLA_RANK:LR_OFF + 2 * GLA_RANK].set(gla_w2[l, 1])
        y_a, y_b = _mixers(main, small, small_t, gb_row, gb_col,
                           mlstm_conv_w[l].astype(F32), row(mlstm_conv_b[l]), row(mlstm_norm[l]),
                           w2.astype(BF16), gla_b[l].reshape(2, 1, QK_W).astype(F32), row(gla_norm[l]))

        x2d, hn = _outproj(y_a, y_b, x2d, w_out[l].astype(BF16), row(norm_mix_post[l]), row(norm_ffn_pre[l]))

        h = _ffn_up(hn, ffn_w_gate[l].astype(BF16), ffn_w_up[l].astype(BF16),
                    ffn_conv_w[l].astype(F32), row(ffn_conv_b[l]))
        g_next = row(norm_mix_pre[l + 1]) if l + 1 < DEPTH else None
        x2d, hn = _ffn_down(h, x2d, ffn_w_down[l].astype(BF16), row(norm_ffn_post[l]), g_next)
    return x2d.reshape(BATCH, SEQ, D_MODEL)
```

```python
import functools

import jax
import jax.numpy as jnp
from jax import lax
from jax.experimental import pallas as pl
from jax.experimental.pallas import tpu as pltpu

F32 = jnp.float32
BF16 = jnp.bfloat16

D_MODEL = 1024
BATCH = 8
SEQ = 2048
DEPTH = 4
TOKENS = BATCH * SEQ
HEADS = 4
DK = 64
DV = 128
QK_W = HEADS * DK
V_W = HEADS * DV
GLA_RANK = 16
GLA_GATE_TEMP = 16.0
D_FF = 2816
EPS = 1e-6
IN_WIDTHS = (2 * QK_W, V_W, V_W, 4 * HEADS, QK_W, QK_W, V_W, V_W, 2 * GLA_RANK)
N_MAIN = 2 * QK_W + V_W + V_W + QK_W + QK_W + V_W + V_W
N_GATES = 4 * HEADS
SMALL_W = 256
LR_OFF = 16

LANES = 128
CHUNK = 128
NCHUNK = SEQ // CHUNK
ROW_BLK = 256
TM = 1024
X_RING = 3
EPI_RB = 256
FF_TN = 256
FF_BLOCKS = (512, 512, 512, 512)
assert sum(FF_BLOCKS) == SEQ
FF_ACT_RB = 128
GLA_LEVELS = (64, 32, 16, 8, 4, 2)
GLA_MILD_LOG_DECAY = 60.0
VMEM_LIMIT = 56 * 1024 * 1024


def _dot(a, b):
    return jnp.dot(a, b, preferred_element_type=F32)


def _dot_nt(a, b):
    return lax.dot_general(a, b, (((1,), (1,)), ((), ())), preferred_element_type=F32)


def _dot_tn(a, b):
    return lax.dot_general(a, b, (((0,), (0,)), ((), ())), preferred_element_type=F32)


def _scan(x, axis, reverse, op, identity):
    n = x.shape[axis]
    idx = lax.broadcasted_iota(jnp.int32, x.shape, axis)
    k = 1
    while k < n:
        if reverse:
            shifted = jnp.where(idx < n - k, pltpu.roll(x, n - k, axis), identity)
        else:
            shifted = jnp.where(idx >= k, pltpu.roll(x, k, axis), identity)
        x = op(x, shifted)
        k *= 2
    return x


def _split_bf16(x):
    hi = x.astype(BF16)
    r1 = x - hi.astype(F32)
    mid = r1.astype(BF16)
    lo = (r1 - mid.astype(F32)).astype(BF16)
    return hi, mid, lo


def _sel_dot(sel, x):
    hi, mid, lo = _split_bf16(x)
    return _dot(sel, hi) + _dot(sel, mid) + _dot(sel, lo)


def _dot_sel(x, sel):
    hi, mid, lo = _split_bf16(x)
    return _dot(hi, sel) + _dot(mid, sel) + _dot(lo, sel)


def _log_sigmoid(x):
    return jnp.minimum(x, 0.0) - jnp.log(1.0 + jnp.exp(-jnp.abs(x)))


def _sigmoid(x):
    return 1.0 / (1.0 + jnp.exp(-x))


def _rms_scale(x):
    return lax.rsqrt(jnp.mean(x * x, axis=-1, keepdims=True) + EPS)


def _project_in(hn, wm_ref, ws_ref, wst_ref, main_ref, small_ref, smallt_ref):
    for j in range(N_MAIN // 512):
        cols = slice(j * 512, (j + 1) * 512)
        main_ref[:, cols] = _dot(hn, wm_ref[:, cols]).astype(BF16)
    small_ref[...] = _dot(hn, ws_ref[...])
    st = _dot_nt(wst_ref[...], hn)
    for j in range(TM // CHUNK):
        smallt_ref[j] = st[:, j * CHUNK:(j + 1) * CHUNK]


def _inproj_kernel(hn_ref, wm_ref, ws_ref, wst_ref, main_ref, small_ref, smallt_ref):
    _project_in(hn_ref[...], wm_ref, ws_ref, wst_ref, main_ref, small_ref, smallt_ref)


def _norm_inproj_kernel(x_ref, g_ref, wm_ref, ws_ref, wst_ref, main_ref, small_ref, smallt_ref):
    x = x_ref[...]
    hn = (x * _rms_scale(x) * g_ref[...]).astype(BF16)
    _project_in(hn, wm_ref, ws_ref, wst_ref, main_ref, small_ref, smallt_ref)


def _inproj(act, w_main, w_small, w_small_t, norm_g=None):
    lead_specs = [pl.BlockSpec((TM, D_MODEL), lambda i: (i, 0))]
    lead_args = [act]
    if norm_g is not None:
        lead_specs.append(pl.BlockSpec((1, D_MODEL), lambda i: (0, 0)))
        lead_args.append(norm_g)
    return pl.pallas_call(
        _inproj_kernel if norm_g is None else _norm_inproj_kernel,
        grid=(TOKENS // TM,),
        in_specs=lead_specs + [
            pl.BlockSpec((D_MODEL, N_MAIN), lambda i: (0, 0)),
            pl.BlockSpec((D_MODEL, SMALL_W), lambda i: (0, 0)),
            pl.BlockSpec((N_GATES, D_MODEL), lambda i: (0, 0)),
        ],
        out_specs=[
            pl.BlockSpec((TM, N_MAIN), lambda i: (i, 0)),
            pl.BlockSpec((TM, SMALL_W), lambda i: (i, 0)),
            pl.BlockSpec((TM // CHUNK, N_GATES, CHUNK), lambda i: (i, 0, 0)),
        ],
        out_shape=[
            jax.ShapeDtypeStruct((TOKENS, N_MAIN), BF16),
            jax.ShapeDtypeStruct((TOKENS, SMALL_W), F32),
            jax.ShapeDtypeStruct((TOKENS // CHUNK, N_GATES, CHUNK), F32),
        ],
        compiler_params=pltpu.CompilerParams(
            dimension_semantics=("arbitrary",), vmem_limit_bytes=VMEM_LIMIT),
        name="inproj" if norm_g is None else "norm_inproj",
    )(*lead_args, w_main, w_small, w_small_t)


def _tri_masks():
    t = lax.broadcasted_iota(jnp.int32, (CHUNK, CHUNK), 0)
    s = lax.broadcasted_iota(jnp.int32, (CHUNK, CHUNK), 1)
    return t, s


def _blk(idx, m):
    return jnp.right_shift(idx, m.bit_length() - 1)


def _half_mask(hh):
    lane = lax.broadcasted_iota(jnp.int32, (CHUNK, LANES), 1)
    return (lane < DK) if hh == 0 else (lane >= DK)


def _accumulate_or_finish(part, sum_sc, rows, pair, finish, gate_ref, nw_ref, y_ref, gate_fn):
    cols2 = slice(pair * 2 * DV, (pair + 1) * 2 * DV)
    if not finish:
        sum_sc[rows, cols2] = part
        return
    total = sum_sc[rows, cols2] + part
    for hh in range(2):
        cols = slice(pair * 2 * DV + hh * DV, pair * 2 * DV + (hh + 1) * DV)
        hs = total[:, hh * DV:(hh + 1) * DV]
        yn = hs * _rms_scale(hs) * nw_ref[:, cols]
        gate = gate_ref[rows, cols].astype(F32)
        y_ref[rows, cols] = (gate_fn(gate) * yn).astype(BF16)


def _mlstm_parts(aqk_ref, av_ref, ao_ref, gcol_ref, grow_ref, gbrow_ref, gbcol_ref, cw_ref, cb_ref, nw_ref,
                 y_ref, q_sc, k_sc, hsum_sc, ct_sc):

    def conv_chunk(c):
        r0 = pl.multiple_of(c * CHUNK, CHUNK)
        cur = aqk_ref[pl.ds(r0, CHUNK), :].astype(F32)
        p0 = pl.multiple_of(jnp.maximum(r0 - 16, 0), 16)
        n0 = pl.multiple_of(jnp.minimum(r0 + CHUNK, SEQ - 16), 16)
        prev_row = aqk_ref[pl.ds(p0, 16), :].astype(F32)[15:16]
        next_row = aqk_ref[pl.ds(n0, 16), :].astype(F32)[0:1]
        prev_row = jnp.where(c == 0, 0.0, prev_row)
        next_row = jnp.where(c == NCHUNK - 1, 0.0, next_row)
        row = lax.broadcasted_iota(jnp.int32, (CHUNK, 1), 0)
        prev = jnp.where(row == 0, prev_row, pltpu.roll(cur, 1, 0))
        nxt = jnp.where(row == CHUNK - 1, next_row, pltpu.roll(cur, CHUNK - 1, 0))
        z = prev * cw_ref[0:1, :] + cur * cw_ref[1:2, :] + nxt * cw_ref[2:3, :] + cb_ref[...]
        act = z * _sigmoid(z)
        return act[:, :QK_W].astype(BF16), (act[:, QK_W:] * (DK ** -0.5)).astype(BF16)

    ct_sc[...] = jnp.zeros_like(ct_sc)

    t_idx, s_idx = _tri_masks()
    le = t_idx >= s_idx
    ge = t_idx <= s_idx
    tri_le = le.astype(BF16)
    tri_ge = ge.astype(BF16)
    lane_row = lax.broadcasted_iota(jnp.int32, (1, LANES), 1)
    half0 = _half_mask(0)
    half0_row = lane_row < DK
    lane2 = lax.broadcasted_iota(jnp.int32, (CHUNK, 2 * DV), 1)
    left_half = lane2 < DV
    bd_r = lax.broadcasted_iota(jnp.int32, (2 * DV, 2 * DV), 0) < DV
    bd_l = lax.broadcasted_iota(jnp.int32, (2 * DV, 2 * DV), 1) < DV
    ones_bd = (bd_r == bd_l).astype(BF16)
    st_r = lax.broadcasted_iota(jnp.int32, (2 * DV, LANES), 0) < DV
    st_l = lax.broadcasted_iota(jnp.int32, (2 * DV, LANES), 1) < DK
    state_mask = st_r == st_l

    def lane_bc(x, j):
        return jnp.broadcast_to(x[:, j:j + 1], (x.shape[0], LANES))

    def one_dir(c, direction, m_vec, n_rows, finish):
        r0 = pl.multiple_of(c * CHUNK, CHUNK)
        rows = pl.ds(r0, CHUNK)
        mask = le if direction == 0 else ge
        t_last = CHUNK - 1 if direction == 0 else 0

        g_row = grow_ref[c] + gbrow_ref[...]
        lf_row = _log_sigmoid(g_row[2 * HEADS:, :])
        b_row = _dot_sel(lf_row, tri_ge if direction == 0 else tri_le)
        a_row = g_row[:2 * HEADS, :] - b_row
        g_col = gcol_ref[rows, :] + gbcol_ref[...]
        lf_col = _log_sigmoid(g_col[:, LANES:])
        b_col = _sel_dot(tri_le if direction == 0 else tri_ge, lf_col)
        a_col = g_col[:, :LANES] - b_col
        pmax = _scan(a_col, 0, direction == 1, jnp.maximum, -jnp.inf)
        u = jnp.maximum(m_vec, pmax)
        u_last = u[t_last:t_last + 1, :]
        m_new = b_col[t_last:t_last + 1, :] + u_last
        thr = jnp.exp(-b_col - u)
        w = jnp.exp(a_col - u_last)
        decay = jnp.exp(m_vec - u_last)

        if not finish:
            q_all, k_all = conv_chunk(c)
            q_sc[rows, :] = q_all
            k_sc[rows, :] = k_all

        n_new = []
        for p in range(HEADS // 2):
            lanes = slice(p * LANES, (p + 1) * LANES)
            hd = [direction * HEADS + 2 * p, direction * HEADS + 2 * p + 1]
            if finish:
                q2 = q_sc[rows, lanes]
                k2 = k_sc[rows, lanes]
            else:
                q2 = q_all[:, lanes]
                k2 = k_all[:, lanes]
            u_bc = [lane_bc(u, j) for j in hd]
            e_pair = jnp.concatenate(
                [jnp.where(mask, jnp.exp(jnp.broadcast_to(a_row[j:j + 1, :], (CHUNK, CHUNK)) - ub), 0.0)
                 for j, ub in zip(hd, u_bc)], axis=1)
            zeros_k = jnp.zeros_like(k2)
            kb = jnp.concatenate([jnp.where(half0, k2, zeros_k), jnp.where(half0, zeros_k, k2)], axis=0)
            pm = (_dot_nt(q2, kb) * e_pair).astype(BF16)
            v_pair = av_ref[rows, p * 2 * DV:(p + 1) * 2 * DV]
            zeros_v = jnp.zeros_like(v_pair)
            v_bd = jnp.concatenate([jnp.where(left_half, v_pair, zeros_v),
                                    jnp.where(left_half, zeros_v, v_pair)], axis=0)
            inter = jnp.where(half0, jnp.exp(lane_bc(m_vec, hd[0]) - u_bc[0]),
                              jnp.exp(lane_bc(m_vec, hd[1]) - u_bc[1]))
            qi = (q2.astype(F32) * inter).astype(BF16)
            ct = ct_sc[direction, p]
            n_row = n_rows[direction * (HEADS // 2) + p]
            n_bd = jnp.where(state_mask, jnp.broadcast_to(n_row, (2 * DV, LANES)), 0.0).astype(BF16)
            num = _dot(pm, v_bd) + _dot_nt(qi, ct.astype(BF16))
            den = _dot(pm, ones_bd) + _dot_nt(qi, n_bd)
            thr_pair = jnp.concatenate([lane_bc(thr, hd[0]), lane_bc(thr, hd[1])], axis=1)
            _accumulate_or_finish(num / jnp.maximum(jnp.abs(den), thr_pair), hsum_sc, rows, p, finish,
                                  ao_ref, nw_ref, y_ref, _sigmoid)
            w_pair = jnp.where(half0, lane_bc(w, hd[0]), lane_bc(w, hd[1]))
            wk = k2.astype(F32) * w_pair
            decay_pair = jnp.where(half0_row, lane_bc(decay, hd[0]), lane_bc(decay, hd[1]))
            upd_t = _dot_tn(v_pair, wk.astype(BF16))
            ct_sc[direction, p] = ct * decay_pair + jnp.where(state_mask, upd_t, 0.0)
            n_new.append(n_row * decay_pair + jnp.sum(wk, axis=0, keepdims=True))
        return m_new, n_new

    def chunk_step(i, carry, finish):
        m_vec, n_rows = carry[0], carry[1:]
        m_fw, n_fw = one_dir(i, 0, m_vec, n_rows, finish)
        m_bw, n_bw = one_dir(NCHUNK - 1 - i, 1, m_vec, n_rows, finish)
        m_next = jnp.where(lane_row < HEADS, m_fw, m_bw)
        return (m_next,) + tuple(n_fw) + tuple(n_bw)

    init = tuple(jnp.zeros((1, LANES), F32) for _ in range(1 + 2 * (HEADS // 2)))
    return init, chunk_step


def _gla_parts(bqk_ref, bv_ref, br_ref, gcol_ref, w2_ref, gb_ref, nw_ref,
               y_ref, la_sc, osum_sc, st_sc):

    def pre_body(r, tot_min):
        rows = pl.ds(pl.multiple_of(r * ROW_BLK, ROW_BLK), ROW_BLK)
        lr = gcol_ref[rows, :LANES].astype(BF16)
        for d in range(2):
            z = _dot(lr, w2_ref[d]) + gb_ref[d]
            la = _log_sigmoid(z) * (1.0 / GLA_GATE_TEMP)
            la_sc[d, rows, :] = la
            for cc in range(ROW_BLK // CHUNK):
                chunk_total = jnp.sum(la[cc * CHUNK:(cc + 1) * CHUNK, :], axis=0, keepdims=True)
                tot_min = jnp.minimum(tot_min, chunk_total)
        return tot_min
    tot_min = lax.fori_loop(0, SEQ // ROW_BLK, pre_body, jnp.zeros((1, QK_W), F32))
    all_mild = jnp.min(tot_min) >= -GLA_MILD_LOG_DECAY

    st_sc[...] = jnp.zeros_like(st_sc)

    t_idx, s_idx = _tri_masks()
    eye = t_idx == s_idx
    lane2 = lax.broadcasted_iota(jnp.int32, (CHUNK, 2 * DV), 1)
    left_half = lane2 < DV
    bd_rows = lax.broadcasted_iota(jnp.int32, (2 * DV, LANES), 0) < DV
    bd_lanes = lax.broadcasted_iota(jnp.int32, (2 * DV, LANES), 1) < DK
    state_mask = bd_rows == bd_lanes

    def intra_mild(q2, k2, b_p, causal):
        qf = (q2 * jnp.exp(b_p)).astype(BF16)
        kf = (k2 * jnp.exp(-b_p)).astype(BF16)
        kb = jnp.concatenate([jnp.where(_half_mask(0), kf, jnp.zeros_like(kf)),
                              jnp.where(_half_mask(1), kf, jnp.zeros_like(kf))], axis=0)
        a_pair = _dot_nt(qf, kb)
        causal2 = jnp.concatenate([causal, causal], axis=1)
        return jnp.where(causal2, a_pair, 0.0)

    def intra_robust(q2, k2, la_p, seg_sum_p, direction, causal, strict_rev):
        k2b = k2.astype(BF16)
        acc = []
        for hh in range(2):
            qm = jnp.where(_half_mask(hh), q2, 0.0).astype(BF16)
            acc.append(jnp.where(eye, _dot_nt(qm, k2b), 0.0))
        for m in GLA_LEVELS + (1,):
            if m == 1:
                dq, dk = la_p, None
            else:
                same = _blk(t_idx, m) == _blk(s_idx, m)
                dq = seg_sum_p(same & causal)
                dk = seg_sum_p(same & strict_rev)
            qt = q2 * jnp.exp(dq)
            ktb = (k2 if dk is None else k2 * jnp.exp(dk)).astype(BF16)
            tb = _blk(t_idx, m)
            sb = _blk(s_idx, m)
            if direction == 0:
                lvl_mask = ((tb & 1) == 1) & (sb == tb - 1)
            else:
                lvl_mask = ((tb & 1) == 0) & (sb == tb + 1)
            for hh in range(2):
                qm = jnp.where(_half_mask(hh), qt, 0.0).astype(BF16)
                acc[hh] = acc[hh] + jnp.where(lvl_mask, _dot_nt(qm, ktb), 0.0)
        return jnp.concatenate(acc, axis=1)

    def one_dir(c, direction, mild, finish):
        r0 = pl.multiple_of(c * CHUNK, CHUNK)
        rows = pl.ds(r0, CHUNK)
        causal = (t_idx >= s_idx) if direction == 0 else (t_idx <= s_idx)
        strict_rev = (s_idx > t_idx) if direction == 0 else (s_idx < t_idx)
        t_last = CHUNK - 1 if direction == 0 else 0

        la = la_sc[direction, rows, :]
        b = _scan(la, 0, direction == 1, jnp.add, 0.0)
        b_tot = b[t_last:t_last + 1, :]
        eb = jnp.exp(b)
        k_decay = jnp.exp(b_tot - b)
        state_decay = jnp.exp(b_tot)

        for p in range(HEADS // 2):
            lanes = slice(p * LANES, (p + 1) * LANES)
            q2 = bqk_ref[rows, lanes].astype(F32) * (DK ** -0.5)
            k2 = bqk_ref[rows, QK_W + p * LANES:QK_W + (p + 1) * LANES].astype(F32)
            if mild:
                a_pair = intra_mild(q2, k2, b[:, lanes], causal)
            else:
                la_p3 = _split_bf16(la[:, lanes])

                def seg_sum_p(sel):
                    selb = sel.astype(BF16)
                    return _dot(selb, la_p3[0]) + _dot(selb, la_p3[1]) + _dot(selb, la_p3[2])
                a_pair = intra_robust(q2, k2, la[:, lanes], seg_sum_p, direction, causal, strict_rev)
            v_pair = bv_ref[rows, p * 2 * DV:(p + 1) * 2 * DV]
            zeros_v = jnp.zeros_like(v_pair)
            v_bd = jnp.concatenate([jnp.where(left_half, v_pair, zeros_v),
                                    jnp.where(left_half, zeros_v, v_pair)], axis=0)
            st = st_sc[direction, p]
            q_inter = (q2 * eb[:, lanes]).astype(BF16)
            o = _dot(a_pair.astype(BF16), v_bd) + _dot_nt(q_inter, st.astype(BF16))
            _accumulate_or_finish(o, osum_sc, rows, p, finish, br_ref, nw_ref, y_ref, lambda g: g * _sigmoid(g))
            k_state = (k2 * k_decay[:, lanes]).astype(BF16)
            upd_t = _dot_tn(v_pair, k_state)
            st_sc[direction, p] = st * state_decay[:, lanes] + jnp.where(state_mask, upd_t, 0.0)

    def chunk_step(i, mild, finish):
        one_dir(i, 0, mild, finish)
        one_dir(NCHUNK - 1 - i, 1, mild, finish)

    return all_mild, chunk_step


def _mixers_kernel(aqk_ref, av_ref, ao_ref, bqk_ref, bv_ref, br_ref, gcol_ref, grow_ref, gbrow_ref, gbcol_ref,
                   cw_ref, cb_ref, nwa_ref, w2_ref, gb_ref, nwb_ref, ya_ref, yb_ref,
                   q_sc, k_sc, hsum_sc, ct_sc, la_sc, osum_sc, st_sc):
    m_init, m_step = _mlstm_parts(aqk_ref, av_ref, ao_ref, gcol_ref, grow_ref, gbrow_ref, gbcol_ref,
                                  cw_ref, cb_ref, nwa_ref, ya_ref, q_sc, k_sc, hsum_sc, ct_sc)
    all_mild, g_step = _gla_parts(bqk_ref, bv_ref, br_ref, gcol_ref, w2_ref, gb_ref, nwb_ref,
                                  yb_ref, la_sc, osum_sc, st_sc)

    def run(mild):
        def half(finish):
            def chunk_body(i, carry):
                g_step(i, mild, finish)
                return m_step(i, carry, finish)
            return chunk_body
        carry = lax.fori_loop(0, NCHUNK // 2, half(False), m_init)
        lax.fori_loop(NCHUNK // 2, NCHUNK, half(True), carry)

    @pl.when(all_mild)
    def _():
        run(True)

    @pl.when(jnp.logical_not(all_mild))
    def _():
        run(False)


def _mixers(main, small, small_t, gb_row, gb_col, conv_w, conv_b, norm_a, w2, gb, norm_b):
    def blk(j):
        return pl.BlockSpec((SEQ, V_W), lambda b: (b, j))
    full = lambda shape: pl.BlockSpec(shape, lambda b: tuple(0 for _ in shape))
    return pl.pallas_call(
        _mixers_kernel,
        grid=(BATCH,),
        in_specs=[
            blk(0), blk(1), blk(2), blk(3), blk(4), blk(5),
            pl.BlockSpec((SEQ, SMALL_W), lambda b: (b, 0)),
            pl.BlockSpec((NCHUNK, N_GATES, CHUNK), lambda b: (b, 0, 0)),
            full((N_GATES, CHUNK)), full((1, SMALL_W)), full((3, 2 * QK_W)), full((1, 2 * QK_W)), full((1, V_W)),
            full((2, LANES, QK_W)), full((2, 1, QK_W)), full((1, V_W)),
        ],
        out_specs=[pl.BlockSpec((SEQ, V_W), lambda b: (b, 0)), pl.BlockSpec((SEQ, V_W), lambda b: (b, 0))],
        out_shape=[jax.ShapeDtypeStruct((TOKENS, V_W), BF16), jax.ShapeDtypeStruct((TOKENS, V_W), BF16)],
        scratch_shapes=[
            pltpu.VMEM((SEQ, QK_W), BF16),
            pltpu.VMEM((SEQ, QK_W), BF16),
            pltpu.VMEM((SEQ, V_W), F32),
            pltpu.VMEM((2, HEADS // 2, 2 * DV, 2 * DK), F32),
            pltpu.VMEM((2, SEQ, QK_W), F32),
            pltpu.VMEM((SEQ, V_W), F32),
            pltpu.VMEM((2, HEADS // 2, 2 * DV, 2 * DK), F32),
        ],
        compiler_params=pltpu.CompilerParams(
            dimension_semantics=("arbitrary",), vmem_limit_bytes=VMEM_LIMIT),
        name="mixers",
    )(main, main, main, main, main, main, small, small_t, gb_row, gb_col, conv_w, conv_b, norm_a, w2, gb, norm_b)


def _residual_and_next_norm(rows, x_ref, branch, g_post_ref, g_next_ref, o_ref, hn_ref):
    x_new = x_ref[rows, :] + branch * _rms_scale(branch) * g_post_ref[...]
    o_ref[rows, :] = x_new
    if hn_ref is not None:
        hn_ref[rows, :] = (x_new * _rms_scale(x_new) * g_next_ref[...]).astype(BF16)


def _row_blocks():
    return [slice(r, r + EPI_RB) for r in range(0, TM, EPI_RB)]


def _outproj_kernel(ya_ref, yb_ref, x_hbm, w_ref, g_post_ref, g_next_ref, o_ref, hn_ref, x_ring, x_sem):
    s = pl.program_id(0)
    n = pl.num_programs(0)

    def x_copy(step):
        slot = lax.rem(step, X_RING)
        return pltpu.make_async_copy(x_hbm.at[pl.ds(pl.multiple_of(step * TM, TM), TM), :],
                                     x_ring.at[slot], x_sem.at[slot])

    @pl.when(s == 0)
    def _():
        x_copy(0).start()
        x_copy(1).start()

    @pl.when(s + 2 < n)
    def _():
        x_copy(s + 2).start()

    x_copy(s).wait()
    x_cur = x_ring.at[lax.rem(s, X_RING)]
    for rows in _row_blocks():
        mix = _dot(ya_ref[rows, :], w_ref[:V_W, :]) + _dot(yb_ref[rows, :], w_ref[V_W:, :])
        _residual_and_next_norm(rows, x_cur, mix, g_post_ref, g_next_ref, o_ref, hn_ref)


def _row_tile(width):
    return pl.BlockSpec((TM, width), lambda i: (i, 0))


def _resident(shape):
    return pl.BlockSpec(shape, lambda i: (0, 0))


def _outproj(ya, yb, x2d, w, g_post, g_next):
    return pl.pallas_call(
        _outproj_kernel,
        grid=(TOKENS // TM,),
        in_specs=[_row_tile(V_W), _row_tile(V_W), pl.BlockSpec(memory_space=pl.ANY),
                  _resident((D_MODEL, D_MODEL)), _resident((1, D_MODEL)), _resident((1, D_MODEL))],
        out_specs=[_row_tile(D_MODEL), _row_tile(D_MODEL)],
        out_shape=[jax.ShapeDtypeStruct((TOKENS, D_MODEL), F32),
                   jax.ShapeDtypeStruct((TOKENS, D_MODEL), BF16)],
        scratch_shapes=[pltpu.VMEM((X_RING, TM, D_MODEL), F32), pltpu.SemaphoreType.DMA((X_RING,))],
        compiler_params=pltpu.CompilerParams(
            dimension_semantics=("arbitrary",), vmem_limit_bytes=VMEM_LIMIT),
        name="outproj",
    )(ya, yb, x2d, w, g_post, g_next)


def _gelu_tanh(x):
    k1 = -2.0 * 0.7978845608028654 * 1.4426950408889634
    return x / (1.0 + jnp.exp2(x * (k1 + (k1 * 0.044715) * (x * x))))


def _ffn_up_kernel(hn_ref, wg_ref, wu_ref, cw_ref, cb_ref, h_ref, g_sc, u_sc):
    g_sc[0:8, :] = jnp.zeros((8, FF_TN), F32)
    g_sc[SEQ + 8:SEQ + 16, :] = jnp.zeros((8, FF_TN), F32)
    n_blk = len(FF_BLOCKS)
    starts = [sum(FF_BLOCKS[:i]) for i in range(n_blk)]

    def project(i):
        rows = slice(starts[i], starts[i] + FF_BLOCKS[i])
        hn = hn_ref[rows, :]
        g_sc[8 + rows.start:8 + rows.stop, :] = _dot(hn, wg_ref[...])
        u_sc[rows, :] = _dot(hn, wu_ref[...])

    row = lax.broadcasted_iota(jnp.int32, (FF_ACT_RB, 1), 0)
    first_row = row == 0
    last_row = row == FF_ACT_RB - 1

    def activate(i):
        for j in range(FF_BLOCKS[i] // FF_ACT_RB):
            r0 = starts[i] + j * FF_ACT_RB
            cur = g_sc[r0 + 8:r0 + 8 + FF_ACT_RB, :]
            prev_row = g_sc[r0 + 7:r0 + 8, :]
            next_row = g_sc[r0 + 8 + FF_ACT_RB:r0 + 9 + FF_ACT_RB, :]
            prev = jnp.where(first_row, prev_row, pltpu.roll(cur, 1, 0))
            nxt = jnp.where(last_row, next_row, pltpu.roll(cur, FF_ACT_RB - 1, 0))
            gc = prev * cw_ref[0:1, :] + cur * cw_ref[1:2, :] + nxt * cw_ref[2:3, :] + cb_ref[...]
            h_ref[r0:r0 + FF_ACT_RB, :] = (_gelu_tanh(gc) * u_sc[r0:r0 + FF_ACT_RB, :]).astype(BF16)

    for i in range(n_blk + 1):
        if i < n_blk:
            project(i)
        if i >= 1:
            activate(i - 1)


def _ffn_up(hn, wg, wu, cw, cb):
    return pl.pallas_call(
        _ffn_up_kernel,
        grid=(BATCH, D_FF // FF_TN),
        in_specs=[
            pl.BlockSpec((SEQ, D_MODEL), lambda b, n: (b, 0)),
            pl.BlockSpec((D_MODEL, FF_TN), lambda b, n: (0, n)),
            pl.BlockSpec((D_MODEL, FF_TN), lambda b, n: (0, n)),
            pl.BlockSpec((3, FF_TN), lambda b, n: (0, n)),
            pl.BlockSpec((1, FF_TN), lambda b, n: (0, n)),
        ],
        out_specs=pl.BlockSpec((SEQ, FF_TN), lambda b, n: (b, n)),
        out_shape=jax.ShapeDtypeStruct((TOKENS, D_FF), BF16),
        scratch_shapes=[
            pltpu.VMEM((SEQ + 16, FF_TN), F32),
            pltpu.VMEM((SEQ, FF_TN), F32),
        ],
        compiler_params=pltpu.CompilerParams(
            dimension_semantics=("arbitrary", "arbitrary"), vmem_limit_bytes=VMEM_LIMIT),
        name="ffn_up",
    )(hn, wg, wu, cw, cb)


def _ffn_down_kernel(h_ref, x_ref, w_ref, g_post_ref, g_next_ref, o_ref, hn_ref):
    for rows in _row_blocks():
        ff = _dot(h_ref[rows, :], w_ref[...])
        _residual_and_next_norm(rows, x_ref, ff, g_post_ref, g_next_ref, o_ref, hn_ref)


def _ffn_down_last_kernel(h_ref, x_ref, w_ref, g_post_ref, o_ref):
    for rows in _row_blocks():
        ff = _dot(h_ref[rows, :], w_ref[...])
        _residual_and_next_norm(rows, x_ref, ff, g_post_ref, None, o_ref, None)


def _ffn_down(h, x2d, w, g_post, g_next):
    last = g_next is None
    in_specs = [_row_tile(D_FF), _row_tile(D_MODEL), _resident((D_FF, D_MODEL)), _resident((1, D_MODEL))]
    out_specs = [_row_tile(D_MODEL)]
    out_shape = [jax.ShapeDtypeStruct((TOKENS, D_MODEL), F32)]
    args = [h, x2d, w, g_post]
    if not last:
        in_specs.append(_resident((1, D_MODEL)))
        out_specs.append(_row_tile(D_MODEL))
        out_shape.append(jax.ShapeDtypeStruct((TOKENS, D_MODEL), BF16))
        args.append(g_next)
    outs = pl.pallas_call(
        _ffn_down_last_kernel if last else _ffn_down_kernel,
        grid=(TOKENS // TM,),
        in_specs=in_specs,
        out_specs=out_specs,
        out_shape=out_shape,
        compiler_params=pltpu.CompilerParams(
            dimension_semantics=("arbitrary",), vmem_limit_bytes=VMEM_LIMIT),
        name="ffn_down_last" if last else "ffn_down",
    )(*args)
    return (outs[0], None) if last else (outs[0], outs[1])


def _split_in_weights(w_in_l):
    offs = [0]
    for w in IN_WIDTHS:
        offs.append(offs[-1] + w)
    a_qk, a_v, a_o, gates, b_q, b_k, b_v, b_r, b_lr = (w_in_l[:, offs[i]:offs[i + 1]] for i in range(9))
    w_main = jnp.concatenate([a_qk, a_v, a_o, b_q, b_k, b_v, b_r], axis=1).astype(BF16)
    gates_i, gates_f = _gates_by_direction_head(gates, axis=1)
    pad = lambda n: jnp.zeros((D_MODEL, n), w_in_l.dtype)
    w_small = jnp.concatenate([gates_i, pad(LR_OFF - 2 * HEADS), b_lr, pad(LANES - LR_OFF - 2 * GLA_RANK),
                               gates_f, pad(SMALL_W - LANES - 2 * HEADS)], axis=1).astype(BF16)
    w_gates_t = jnp.concatenate([gates_i, gates_f], axis=1).T.astype(BF16)
    return w_main, w_small, w_gates_t


def _gates_by_direction_head(g, axis):
    i_fw, f_fw, i_bw, f_bw = jnp.split(g, 4, axis=axis)
    return jnp.concatenate([i_fw, i_bw], axis=axis), jnp.concatenate([f_fw, f_bw], axis=axis)


def kernel(x, norm_mix_pre, norm_mix_post, norm_ffn_pre, norm_ffn_post, w_in, mlstm_gate_b, mlstm_conv_w,
           mlstm_conv_b, mlstm_norm, gla_w2, gla_b, gla_norm, w_out, ffn_w_gate, ffn_w_up, ffn_conv_w,
           ffn_conv_b, ffn_w_down):
    def row(v):
        return v.reshape(1, -1).astype(F32)

    x2d = x.reshape(TOKENS, D_MODEL).astype(F32)
    hn = None
    for l in range(DEPTH):
        w_main, w_small, w_small_t = _split_in_weights(w_in[l])
        if l == 0:
            main, small, small_t = _inproj(x2d, w_main, w_small, w_small_t, norm_g=row(norm_mix_pre[0]))
        else:
            main, small, small_t = _inproj(hn, w_main, w_small, w_small_t)

        bias_i, bias_f = _gates_by_direction_head(mlstm_gate_b[l].astype(F32), axis=0)
        gb_row = jnp.broadcast_to(jnp.concatenate([bias_i, bias_f])[:, None], (N_GATES, CHUNK))
        gb_col = jnp.concatenate([bias_i, jnp.zeros((LANES - 2 * HEADS,), F32),
                                  bias_f, jnp.zeros((SMALL_W - LANES - 2 * HEADS,), F32)]).reshape(1, SMALL_W)
        w2 = jnp.zeros((2, LANES, QK_W), F32)
        w2 = w2.at[0, LR_OFF:LR_OFF + GLA_RANK].set(gla_w2[l, 0])
        w2 = w2.at[1, LR_OFF + GLA_RANK:LR_OFF + 2 * GLA_RANK].set(gla_w2[l, 1])
        y_a, y_b = _mixers(main, small, small_t, gb_row, gb_col,
                           mlstm_conv_w[l].astype(F32), row(mlstm_conv_b[l]), row(mlstm_norm[l]),
                           w2.astype(BF16), gla_b[l].reshape(2, 1, QK_W).astype(F32), row(gla_norm[l]))

        x2d, hn = _outproj(y_a, y_b, x2d, w_out[l].astype(BF16), row(norm_mix_post[l]), row(norm_ffn_pre[l]))

        h = _ffn_up(hn, ffn_w_gate[l].astype(BF16), ffn_w_up[l].astype(BF16),
                    ffn_conv_w[l].astype(F32), row(ffn_conv_b[l]))
        g_next = row(norm_mix_pre[l + 1]) if l + 1 < DEPTH else None
        x2d, hn = _ffn_down(h, x2d, ffn_w_down[l].astype(BF16), row(norm_ffn_post[l]), g_next)
    return x2d.reshape(BATCH, SEQ, D_MODEL)
```
